```python
import math
import jax, jax.numpy as jnp
from jax import lax
import numpy as np

D_MODEL = 2048
BATCH = 8
SEQ = 2048
DEPTH = 4
DEC_BATCH = 2
DEC_SEQ = 8192
PAST_LEN = 128

GRID_W = 64
Q_BLOCK = 128
RET_CHUNK = 128
ROPE_THETA = 10000.0
NORM_EPS = 1e-6
GN_EPS = 1e-5
A_HEADS = 8
A_KV_HEADS = 2
A_HEAD_DIM = 128
B_HEADS = 8
B_Q_RANK = 512
B_KV_RANK = 256
B_NOPE_DIM = 128
B_ROPE_DIM = 64
B_V_DIM = 128
C_HEADS = 8
C_QK_DIM = 256
C_V_DIM = 512
D_FF = 4 * D_MODEL
N_EVEN = (DEPTH + 1) // 2
N_ODD = DEPTH // 2
EVEN_SPLITS = (A_HEADS * A_HEAD_DIM, A_KV_HEADS * A_HEAD_DIM, A_KV_HEADS * A_HEAD_DIM, B_Q_RANK, B_KV_RANK, B_ROPE_DIM)
EVEN_IN = sum(EVEN_SPLITS)
EVEN_OUT = A_HEADS * A_HEAD_DIM + B_HEADS * B_V_DIM
ODD_SPLITS = (C_HEADS * C_QK_DIM, C_HEADS * C_QK_DIM, C_HEADS * C_V_DIM, C_HEADS * C_V_DIM)
ODD_IN = sum(ODD_SPLITS)
ODD_OUT = C_HEADS * C_V_DIM

kernel_name = 'hybrid_axial_gqa_mla_retention_encoder'


def _split(a, sizes):
    offs = np.cumsum(sizes)[:-1]
    return jnp.split(a, [int(o) for o in offs], axis=-1)


def _rmsnorm(x, g):
    xf = x.astype(jnp.float32)
    y = xf * lax.rsqrt(jnp.mean(xf * xf, axis=-1, keepdims=True) + NORM_EPS)
    return (y * g.astype(jnp.float32)).astype(x.dtype)


def _axial_rope(S, d_rot):
    rows = S // GRID_W
    row = jnp.repeat(jnp.arange(rows, dtype=jnp.float32), GRID_W)
    col = jnp.tile(jnp.arange(GRID_W, dtype=jnp.float32), rows)
    n = d_rot // 4
    inv = ROPE_THETA ** (-jnp.arange(n, dtype=jnp.float32) / n)
    ang = jnp.concatenate([row[:, None] * inv, col[:, None] * inv], axis=-1)
    return jnp.cos(ang), jnp.sin(ang)


def _apply_rope(x, cos, sin):
    d = x.shape[-1]
    xf = x.astype(jnp.float32).reshape(x.shape[:-1] + (d // 2, 2))
    x0, x1 = xf[..., 0], xf[..., 1]
    c = cos[None, :, None, :]
    s = sin[None, :, None, :]
    out = jnp.stack([x0 * c - x1 * s, x0 * s + x1 * c], axis=-1).reshape(x.shape)
    return out.astype(x.dtype)


def _block_attention(q, k, v, scale):
    B, S, H, dq = q.shape
    Hk = k.shape[2]
    G = H // Hk
    dv = v.shape[-1]
    nb = S // Q_BLOCK
    qb = q.reshape(B, nb, Q_BLOCK, Hk, G, dq).transpose(1, 0, 3, 4, 2, 5)

    def one(qblk):
        s = jnp.einsum('bkgqd,bskd->bkgqs', qblk, k).astype(jnp.float32) * scale
        p = jax.nn.softmax(s, axis=-1)
        return jnp.einsum('bkgqs,bskd->bkgqd', p.astype(v.dtype), v)

    o = lax.map(one, qb)
    return o.transpose(1, 0, 4, 2, 3, 5).reshape(B, S, H * dv)


def _even_mixer(h, rope_a, rope_b, w_in, g_qa, g_ka, g_cq, w_uq, g_ckv, w_ukv, w_out):
    B, S, _ = h.shape
    qa, ka, va, cq, ckv, kr = _split(h @ w_in, EVEN_SPLITS)
    qa = _apply_rope(_rmsnorm(qa.reshape(B, S, A_HEADS, A_HEAD_DIM), g_qa), *rope_a)
    ka = _apply_rope(_rmsnorm(ka.reshape(B, S, A_KV_HEADS, A_HEAD_DIM), g_ka), *rope_a)
    va = va.reshape(B, S, A_KV_HEADS, A_HEAD_DIM)
    out_a = _block_attention(qa, ka, va, A_HEAD_DIM ** -0.5)
    qb = (_rmsnorm(cq, g_cq) @ w_uq).reshape(B, S, B_HEADS, B_NOPE_DIM + B_ROPE_DIM)
    kvb = (_rmsnorm(ckv, g_ckv) @ w_ukv).reshape(B, S, B_HEADS, B_NOPE_DIM + B_V_DIM)
    kb_nope, vb = kvb[..., :B_NOPE_DIM], kvb[..., B_NOPE_DIM:]
    kb_rope = _apply_rope(kr.reshape(B, S, 1, B_ROPE_DIM), *rope_b)
    qb = jnp.concatenate([qb[..., :B_NOPE_DIM], _apply_rope(qb[..., B_NOPE_DIM:], *rope_b)], axis=-1)
    kb = jnp.concatenate([kb_nope, jnp.broadcast_to(kb_rope, (B, S, B_HEADS, B_ROPE_DIM))], axis=-1)
    out_b = _block_attention(qb, kb, vb, (B_NOPE_DIM + B_ROPE_DIM) ** -0.5)
    return jnp.concatenate([out_a, out_b], axis=-1) @ w_out


def _retention_scan(q, k, v, log_g, include_diag):
    B, S, H, DK = q.shape
    DV = v.shape[-1]
    nc = S // RET_CHUNK

    def chunks(a):
        return a.reshape(B, nc, RET_CHUNK, H, a.shape[-1]).transpose(1, 0, 3, 2, 4)

    idx = jnp.arange(RET_CHUNK, dtype=jnp.float32)
    rel = idx[:, None] - idx[None, :]
    mask = (rel >= 0) if include_diag else (rel > 0)
    d_intra = jnp.where(mask, jnp.exp(jnp.maximum(rel, 0.0)[None] * log_g[:, None, None]), 0.0)
    q_dec = jnp.exp((idx + 1.0)[None, :] * log_g[:, None])[:, :, None]
    k_dec = jnp.exp((RET_CHUNK - 1.0 - idx)[None, :] * log_g[:, None])[:, :, None]
    c_dec = jnp.exp(RET_CHUNK * log_g)[:, None, None]

    def step(state, qkv):
        qc, kc, vc = qkv
        attn = jnp.einsum('bhid,bhjd->bhij', qc, kc) * d_intra
        out = jnp.einsum('bhij,bhjv->bhiv', attn, vc) + jnp.einsum('bhid,bhdv->bhiv', qc * q_dec, state)
        state = state * c_dec + jnp.einsum('bhjd,bhjv->bhdv', kc * k_dec, vc)
        return state, out

    state0 = jnp.zeros((B, H, DK, DV), jnp.float32)
    _, out = lax.scan(step, state0, (chunks(q), chunks(k), chunks(v)))
    return out.transpose(1, 0, 3, 2, 4).reshape(B, S, H, DV)


def _odd_mixer(h, rope_c, w_in, decay_f, decay_b, w_out):
    B, S, _ = h.shape
    q, k, v, g = _split(h @ w_in, ODD_SPLITS)
    q = _apply_rope(q.reshape(B, S, C_HEADS, C_QK_DIM), *rope_c).astype(jnp.float32)
    k = _apply_rope(k.reshape(B, S, C_HEADS, C_QK_DIM), *rope_c).astype(jnp.float32) * (C_QK_DIM ** -0.5)
    v = v.reshape(B, S, C_HEADS, C_V_DIM).astype(jnp.float32)
    lg_f = jax.nn.log_sigmoid(decay_f.astype(jnp.float32))
    lg_b = jax.nn.log_sigmoid(decay_b.astype(jnp.float32))
    flip = lambda a: jnp.flip(a, axis=1)
    y = _retention_scan(q, k, v, lg_f, True) + flip(_retention_scan(flip(q), flip(k), flip(v), lg_b, False))
    mu = jnp.mean(y, axis=-1, keepdims=True)
    var = jnp.mean(jnp.square(y - mu), axis=-1, keepdims=True)
    y = ((y - mu) * lax.rsqrt(var + GN_EPS)).reshape(B, S, ODD_OUT).astype(h.dtype)
    return (jax.nn.silu(g) * y) @ w_out


def _trunk(x, g_mix_pre, g_mix_post, g_mlp_pre, g_mlp_post, w_in_even, g_qnorm_a, g_knorm_a, g_cq_b, w_uq_b,
           g_ckv_b, w_ukv_b, w_out_even, w_in_odd, decay_fwd, decay_bwd, w_out_odd, w_mlp_up, w_mlp_down):
    S = x.shape[1]
    rope_a = _axial_rope(S, A_HEAD_DIM)
    rope_b = _axial_rope(S, B_ROPE_DIM)
    rope_c = _axial_rope(S, C_QK_DIM)
    for layer in range(DEPTH):
        hn = _rmsnorm(x, g_mix_pre[layer])
        if layer % 2 == 0:
            e = layer // 2
            m = _even_mixer(hn, rope_a, rope_b, w_in_even[e], g_qnorm_a[e], g_knorm_a[e], g_cq_b[e], w_uq_b[e],
                            g_ckv_b[e], w_ukv_b[e], w_out_even[e])
        else:
            o = layer // 2
            m = _odd_mixer(hn, rope_c, w_in_odd[o], decay_fwd[o], decay_bwd[o], w_out_odd[o])
        x = x + _rmsnorm(m, g_mix_post[layer])
        hn = _rmsnorm(x, g_mlp_pre[layer])
        f = jnp.square(jax.nn.relu(hn @ w_mlp_up[layer])) @ w_mlp_down[layer]
        x = x + _rmsnorm(f, g_mlp_post[layer])
    return x


def setup_inputs(seed: int = 0) -> dict:
    key = jax.random.key(seed)
    ks = jax.random.split(key, 20)

    def w(k, shape, fan_in):
        return jax.random.normal(k, shape, jnp.float32) * (fan_in ** -0.5)

    def gain(k, shape):
        return 1.0 + 0.05 * jax.random.normal(k, shape, jnp.float32)

    gamma = 1.0 - 2.0 ** (-5.0 - np.arange(C_HEADS))
    ret_logit = jnp.asarray(np.log(gamma / (1.0 - gamma)).astype(np.float32))
    return {
        'x_prompt': jax.random.normal(ks[0], (BATCH, SEQ, D_MODEL), jnp.float32),
        'x_sample': jax.random.normal(ks[1], (DEC_BATCH, DEC_SEQ, D_MODEL), jnp.float32),
        'g_mix_pre': gain(ks[2], (DEPTH, D_MODEL)),
        'g_mix_post': gain(ks[3], (DEPTH, D_MODEL)),
        'g_mlp_pre': gain(ks[4], (DEPTH, D_MODEL)),
        'g_mlp_post': gain(ks[5], (DEPTH, D_MODEL)),
        'w_in_even': w(ks[6], (N_EVEN, D_MODEL, EVEN_IN), D_MODEL),
        'g_qnorm_a': gain(ks[7], (N_EVEN, A_HEAD_DIM)),
        'g_knorm_a': gain(ks[8], (N_EVEN, A_HEAD_DIM)),
        'g_cq_b': gain(ks[9], (N_EVEN, B_Q_RANK)),
        'w_uq_b': w(ks[10], (N_EVEN, B_Q_RANK, B_HEADS * (B_NOPE_DIM + B_ROPE_DIM)), B_Q_RANK),
        'g_ckv_b': gain(ks[11], (N_EVEN, B_KV_RANK)),
        'w_ukv_b': w(ks[12], (N_EVEN, B_KV_RANK, B_HEADS * (B_NOPE_DIM + B_V_DIM)), B_KV_RANK),
        'w_out_even': w(ks[13], (N_EVEN, EVEN_OUT, D_MODEL), EVEN_OUT),
        'w_in_odd': w(ks[14], (N_ODD, D_MODEL, ODD_IN), D_MODEL),
        'decay_fwd': ret_logit[None, :] + 0.1 * jax.random.normal(ks[15], (N_ODD, C_HEADS), jnp.float32),
        'decay_bwd': ret_logit[None, :] + 0.1 * jax.random.normal(ks[16], (N_ODD, C_HEADS), jnp.float32),
        'w_out_odd': w(ks[17], (N_ODD, ODD_OUT, D_MODEL), ODD_OUT),
        'w_mlp_up': w(ks[18], (DEPTH, D_MODEL, D_FF), D_MODEL),
        'w_mlp_down': w(ks[19], (DEPTH, D_FF, D_MODEL), D_FF),
    }


def reference(x_prompt, x_sample, g_mix_pre, g_mix_post, g_mlp_pre, g_mlp_post, w_in_even, g_qnorm_a, g_knorm_a,
              g_cq_b, w_uq_b, g_ckv_b, w_ukv_b, w_out_even, w_in_odd, decay_fwd, decay_bwd, w_out_odd,
              w_mlp_up, w_mlp_down):
    y_prompt = _trunk(x_prompt, g_mix_pre, g_mix_post, g_mlp_pre, g_mlp_post, w_in_even, g_qnorm_a, g_knorm_a,
                      g_cq_b, w_uq_b, g_ckv_b, w_ukv_b, w_out_even, w_in_odd, decay_fwd, decay_bwd, w_out_odd,
                      w_mlp_up, w_mlp_down)
    y_sample = _trunk(x_sample, g_mix_pre, g_mix_post, g_mlp_pre, g_mlp_post, w_in_even, g_qnorm_a, g_knorm_a,
                      g_cq_b, w_uq_b, g_ckv_b, w_ukv_b, w_out_even, w_in_odd, decay_fwd, decay_bwd, w_out_odd,
                      w_mlp_up, w_mlp_down)
    return (y_prompt, y_sample)
```

```python
import functools
import math

import jax
import jax.numpy as jnp
import numpy as np
from jax import lax
from jax.experimental import pallas as pl
from jax.experimental.pallas import tpu as pltpu

D_MODEL = 2048
DEPTH = 4
GRID_W = 64
ROPE_THETA = 10000.0
NORM_EPS = 1e-6
GN_EPS = 1e-5
A_HEADS, A_KV_HEADS, A_HEAD_DIM = 8, 2, 128
B_HEADS, B_Q_RANK, B_KV_RANK, B_NOPE_DIM, B_ROPE_DIM, B_V_DIM = 8, 512, 256, 128, 64, 128
C_HEADS, C_QK_DIM, C_V_DIM = 8, 256, 512
D_FF = 4 * D_MODEL

LANES = 128
VMEM_LIMIT_BYTES = 56 * 1024 * 1024

BF16 = jnp.bfloat16
F32 = jnp.float32


def _cparams(semantics):
    return pltpu.CompilerParams(dimension_semantics=semantics, vmem_limit_bytes=VMEM_LIMIT_BYTES)


def _rms(x, g):
    return x * lax.rsqrt(jnp.mean(x * x, axis=-1, keepdims=True) + NORM_EPS) * g


def _dot(a, b):
    return jnp.dot(a, b, preferred_element_type=F32)


def _dot_nt(a, b):
    return lax.dot_general(a, b, (((1,), (1,)), ((), ())), preferred_element_type=F32)


def _dot_tn(a, b):
    return lax.dot_general(a, b, (((0,), (0,)), ((), ())), preferred_element_type=F32)


def _rope_half(x, c, s):
    return x * c + pltpu.roll(x, 64, 1) * s


EVEN_W1_COLS = 1024 + 256 + 256 + 512 + 256 + 128


def _even_proj_kernel(x_ref, gpre_ref, w1_ref, gqa_ref, gka_ref, gcq_ref, gckv_ref, wuq_ref, wukv_ref,
                      ca_ref, sa_ref, cb_ref, sb_ref,
                      qa_ref, ka_ref, va_ref, qbn_ref, qbr_ref, kbn_ref, vb_ref, kr_ref):
    hn = _rms(x_ref[...], gpre_ref[...]).astype(BF16)
    p = _dot(hn, w1_ref[...])
    ca, sa = ca_ref[...], sa_ref[...]
    cb, sb = cb_ref[...], sb_ref[...]
    gq = gqa_ref[...] * (A_HEAD_DIM ** -0.5)
    gk = gka_ref[...]
    for h in range(A_HEADS):
        xh = p[:, h * 128:(h + 1) * 128]
        qa_ref[:, h * 128:(h + 1) * 128] = _rope_half(_rms(xh, gq), ca, sa).astype(BF16)
    for h in range(A_KV_HEADS):
        xh = p[:, 1024 + h * 128:1024 + (h + 1) * 128]
        ka_ref[:, h * 128:(h + 1) * 128] = _rope_half(_rms(xh, gk), ca, sa).astype(BF16)
    va_ref[...] = p[:, 1280:1536].astype(BF16)
    cqn = _rms(p[:, 1536:2048], gcq_ref[...]).astype(BF16)
    qb = _dot(cqn, wuq_ref[...]) * ((B_NOPE_DIM + B_ROPE_DIM) ** -0.5)
    qbn_ref[...] = qb[:, :1024].astype(BF16)
    for h in range(B_HEADS):
        xh = qb[:, 1024 + h * 128:1024 + (h + 1) * 128]
        qbr_ref[:, h * 128:(h + 1) * 128] = _rope_half(xh, cb, sb).astype(BF16)
    ckvn = _rms(p[:, 2048:2304], gckv_ref[...]).astype(BF16)
    kv = _dot(ckvn, wukv_ref[...])
    kbn_ref[...] = kv[:, :1024].astype(BF16)
    vb_ref[...] = kv[:, 1024:].astype(BF16)
    kr_ref[...] = _rope_half(p[:, 2304:2432], cb, sb).astype(BF16)


def _even_proj(x, gpre, w1, gqa, gka, gcq, gckv, wuq, wukv, ca, sa, cb, sb, *, seq, tm):
    T = x.shape[0]
    nblk_seq = seq // tm
    row = lambda i: (i, 0)
    fixed = lambda i: (0, 0)
    pos = lambda i: (i % nblk_seq, 0)
    out_cols = (1024, 256, 256, 1024, 1024, 1024, 1024, 128)
    return pl.pallas_call(
        _even_proj_kernel,
        out_shape=[jax.ShapeDtypeStruct((T, c), BF16) for c in out_cols],
        grid=(T // tm,),
        in_specs=[
            pl.BlockSpec((tm, D_MODEL), row),
            pl.BlockSpec((1, D_MODEL), fixed),
            pl.BlockSpec((D_MODEL, EVEN_W1_COLS), fixed, pipeline_mode=pl.Buffered(1)),
            pl.BlockSpec((1, 128), fixed),
            pl.BlockSpec((1, 128), fixed),
            pl.BlockSpec((1, B_Q_RANK), fixed),
            pl.BlockSpec((1, B_KV_RANK), fixed),
            pl.BlockSpec((B_Q_RANK, 2048), fixed, pipeline_mode=pl.Buffered(1)),
            pl.BlockSpec((B_KV_RANK, 2048), fixed, pipeline_mode=pl.Buffered(1)),
            pl.BlockSpec((tm, 128), pos),
            pl.BlockSpec((tm, 128), pos),
            pl.BlockSpec((tm, 128), pos),
            pl.BlockSpec((tm, 128), pos),
        ],
        out_specs=[pl.BlockSpec((tm, c), row) for c in out_cols],
        compiler_params=_cparams(("arbitrary",)),
        name="even_proj",
    )(x, gpre, w1, gqa, gka, gcq, gckv, wuq, wukv, ca, sa, cb, sb)


def _attn_kernel(*refs, G, rope_ext, nk):
    if rope_ext:
        q_ref, qr_ref, k_ref, kr_ref, v_ref, o_ref = refs[:6]
        scratch = refs[6:]
    else:
        q_ref, k_ref, v_ref, o_ref = refs[:4]
        qr_ref = kr_ref = None
        scratch = refs[4:]
    ki = pl.program_id(3)
    k = k_ref[...]
    if rope_ext:
        k = jnp.concatenate([k, kr_ref[...]], axis=-1)
    v = v_ref[...]

    def scores(g):
        q = q_ref[:, g * 128:(g + 1) * 128]
        if rope_ext:
            q = jnp.concatenate([q, qr_ref[:, g * 128:(g + 1) * 128]], axis=-1)
        return _dot_nt(q, k)

    if nk == 1:
        for g in range(G):
            s = scores(g)
            m = jnp.max(s, axis=-1, keepdims=True)
            p = jnp.exp(s - m)
            l = jnp.sum(p, axis=-1, keepdims=True)
            o = _dot(p.astype(BF16), v)
            o_ref[:, g * 128:(g + 1) * 128] = (o / l).astype(o_ref.dtype)
        return

    m_sc, l_sc, acc_sc = scratch

    @pl.when(ki == 0)
    def _():
        m_sc[...] = jnp.full(m_sc.shape, -jnp.inf, F32)
        l_sc[...] = jnp.zeros(l_sc.shape, F32)
        acc_sc[...] = jnp.zeros(acc_sc.shape, F32)

    for g in range(G):
        s = scores(g)
        m_prev = m_sc[g]
        m_new = jnp.maximum(m_prev, jnp.max(s, axis=-1, keepdims=True))
        alpha = jnp.exp(m_prev - m_new)
        p = jnp.exp(s - m_new)
        l_sc[g] = alpha * l_sc[g] + jnp.sum(p, axis=-1, keepdims=True)
        acc_sc[g] = alpha * acc_sc[g] + _dot(p.astype(BF16), v)
        m_sc[g] = m_new

    @pl.when(ki == nk - 1)
    def _():
        for g in range(G):
            o_ref[:, g * 128:(g + 1) * 128] = (acc_sc[g] / l_sc[g]).astype(o_ref.dtype)


def _attention(q, k, v, *, batch, seq, n_kv, G, tq, tk, qr=None, kr=None):
    T = q.shape[0]
    nq, nk = seq // tq, seq // tk
    rope_ext = qr is not None
    q_map = lambda b, h, qi, ki: (b * nq + qi, h)
    kv_map = lambda b, h, qi, ki: (b * nk + ki, h)
    kr_map = lambda b, h, qi, ki: (b * nk + ki, 0)
    in_specs = [pl.BlockSpec((tq, G * 128), q_map)]
    args = [q]
    if rope_ext:
        in_specs.append(pl.BlockSpec((tq, G * 128), q_map))
        args.append(qr)
    in_specs.append(pl.BlockSpec((tk, 128), kv_map))
    args.append(k)
    if rope_ext:
        in_specs.append(pl.BlockSpec((tk, 128), kr_map))
        args.append(kr)
    in_specs.append(pl.BlockSpec((tk, 128), kv_map))
    args.append(v)
    scratch = []
    if nk > 1:
        scratch = [pltpu.VMEM((G, tq, 1), F32), pltpu.VMEM((G, tq, 1), F32), pltpu.VMEM((G, tq, 128), F32)]
    return pl.pallas_call(
        functools.partial(_attn_kernel, G=G, rope_ext=rope_ext, nk=nk),
        out_shape=jax.ShapeDtypeStruct((T, n_kv * G * 128), BF16),
        grid=(batch, n_kv, nq, nk),
        in_specs=in_specs,
        out_specs=pl.BlockSpec((tq, G * 128), q_map),
        scratch_shapes=scratch,
        compiler_params=_cparams(("arbitrary",) * 4),
        name="attn_mla" if rope_ext else "attn_gqa",
    )(*args)


def _out_proj_kernel(*refs, n_k):
    x_ref, g_ref, w_ref = refs[0], refs[1], refs[2]
    a_refs = refs[3:3 + n_k]
    o_ref = refs[3 + n_k]
    kk = pl.program_id(1)
    for i in range(n_k):
        @pl.when(kk == i)
        def _(i=i):
            part = _dot(a_refs[i][...], w_ref[...])
            if i == 0:
                o_ref[...] = part
            else:
                o_ref[...] += part

    @pl.when(kk == n_k - 1)
    def _():
        o_ref[...] = x_ref[...] + _rms(o_ref[...], g_ref[...])


def _out_proj(x, g, w, a_list, *, tm, tk):
    T = x.shape[0]
    n_k = len(a_list)
    row = lambda i, k: (i, 0)
    return pl.pallas_call(
        functools.partial(_out_proj_kernel, n_k=n_k),
        out_shape=jax.ShapeDtypeStruct((T, D_MODEL), F32),
        grid=(T // tm, n_k),
        in_specs=[
            pl.BlockSpec((tm, D_MODEL), row),
            pl.BlockSpec((1, D_MODEL), lambda i, k: (0, 0)),
            pl.BlockSpec((tk, D_MODEL), lambda i, k: (k, 0)),
        ] + [pl.BlockSpec((tm, tk), functools.partial(lambda i, k, cb: (i, cb), cb=cb)) for _, cb in a_list],
        out_specs=pl.BlockSpec((tm, D_MODEL), row),
        compiler_params=_cparams(("arbitrary", "arbitrary")),
        name="out_proj",
    )(x, g, w, *[a for a, _ in a_list])


def _mlp_kernel(x_ref, gpre_ref, gpost_ref, wup_ref, wdn_ref, o_ref, hn_sc, *, nf):
    f = pl.program_id(1)

    @pl.when(f == 0)
    def _():
        hn_sc[...] = _rms(x_ref[...], gpre_ref[...]).astype(BF16)

    h = jnp.maximum(_dot(hn_sc[...], wup_ref[...]), 0.0)
    part = _dot((h * h).astype(BF16), wdn_ref[...])

    @pl.when(f == 0)
    def _():
        o_ref[...] = part

    @pl.when(f > 0)
    def _():
        o_ref[...] += part

    @pl.when(f == nf - 1)
    def _():
        o_ref[...] = x_ref[...] + _rms(o_ref[...], gpost_ref[...])


def _mlp(x, gpre, gpost, wup, wdn, *, tm, tf):
    T = x.shape[0]
    nf = D_FF // tf
    row = lambda i, f: (i, 0)
    fixed = lambda i, f: (0, 0)
    return pl.pallas_call(
        functools.partial(_mlp_kernel, nf=nf),
        out_shape=jax.ShapeDtypeStruct((T, D_MODEL), F32),
        grid=(T // tm, nf),
        in_specs=[
            pl.BlockSpec((tm, D_MODEL), row),
            pl.BlockSpec((1, D_MODEL), fixed),
            pl.BlockSpec((1, D_MODEL), fixed),
            pl.BlockSpec((D_MODEL, tf), lambda i, f: (0, f)),
            pl.BlockSpec((tf, D_MODEL), lambda i, f: (f, 0)),
        ],
        out_specs=pl.BlockSpec((tm, D_MODEL), row),
        scratch_shapes=[pltpu.VMEM((tm, D_MODEL), BF16)],
        compiler_params=_cparams(("arbitrary", "arbitrary")),
        name="mlp",
    )(x, gpre, gpost, wup, wdn)


ODD_TN = 1024
ODD_QK_BLOCKS = 2 * C_HEADS * C_QK_DIM // ODD_TN


def _odd_proj_kernel(x_ref, gpre_ref, w_ref, c_ref, s_ref, o_ref, hn_sc):
    j = pl.program_id(1)

    @pl.when(j == 0)
    def _():
        hn_sc[...] = _rms(x_ref[...], gpre_ref[...]).astype(BF16)

    acc = _dot(hn_sc[...], w_ref[...])

    @pl.when(j < ODD_QK_BLOCKS)
    def _():
        scale = jnp.where(j < ODD_QK_BLOCKS // 2, 1.0, C_QK_DIM ** -0.5).astype(F32)
        c = c_ref[...] * scale
        s = s_ref[...] * scale
        for h in range(ODD_TN // C_QK_DIM):
            x0 = acc[:, h * 256:h * 256 + 128]
            x1 = acc[:, h * 256 + 128:(h + 1) * 256]
            o_ref[:, h * 256:h * 256 + 128] = (x0 * c - x1 * s).astype(BF16)
            o_ref[:, h * 256 + 128:(h + 1) * 256] = (x0 * s + x1 * c).astype(BF16)

    @pl.when(j >= ODD_QK_BLOCKS)
    def _():
        o_ref[...] = acc.astype(BF16)


def _odd_proj(x, gpre, w, cc, sc, *, seq, tm):
    T = x.shape[0]
    N = w.shape[1]
    nblk_seq = seq // tm
    return pl.pallas_call(
        _odd_proj_kernel,
        out_shape=jax.ShapeDtypeStruct((T, N), BF16),
        grid=(T // tm, N // ODD_TN),
        in_specs=[
            pl.BlockSpec((tm, D_MODEL), lambda i, j: (i, 0)),
            pl.BlockSpec((1, D_MODEL), lambda i, j: (0, 0)),
            pl.BlockSpec((D_MODEL, ODD_TN), lambda i, j: (0, j)),
            pl.BlockSpec((tm, 128), lambda i, j: (i % nblk_seq, 0)),
            pl.BlockSpec((tm, 128), lambda i, j: (i % nblk_seq, 0)),
        ],
        out_specs=pl.BlockSpec((tm, ODD_TN), lambda i, j: (i, j)),
        scratch_shapes=[pltpu.VMEM((tm, D_MODEL), BF16)],
        compiler_params=_cparams(("arbitrary", "arbitrary")),
        name="odd_proj",
    )(x, gpre, w, cc, sc)


def _ret_fwd_kernel(lgf_ref, lgb_ref, q_ref, k_ref, v_ref, y_ref, st_sc, d_sc, qd_sc, kd_sc, cd_sc, *, C):
    h = pl.program_id(1)
    c = pl.program_id(2)
    lgf = lgf_ref[h]
    lgb = lgb_ref[h]

    @pl.when(c == 0)
    def _():
        st_sc[...] = jnp.zeros(st_sc.shape, F32)
        cd_sc[...] = jnp.exp(jnp.full(cd_sc.shape, C, F32) * lgf)
        i = lax.broadcasted_iota(jnp.int32, (C, C), 0)
        j = lax.broadcasted_iota(jnp.int32, (C, C), 1)
        rel = (i - j).astype(F32)
        d_sc[...] = jnp.where(rel >= 0, jnp.exp(rel * lgf), jnp.exp(-rel * lgb))
        idx = lax.broadcasted_iota(jnp.int32, (C, C_QK_DIM), 0).astype(F32)
        qd_sc[...] = jnp.exp((idx + 1.0) * lgf)
        kd_sc[...] = jnp.exp((C - 1.0 - idx) * lgf)

    q = q_ref[...]
    k = k_ref[...]
    v = v_ref[...]
    a = (_dot_nt(q, k) * d_sc[...]).astype(BF16)
    qd = (q.astype(F32) * qd_sc[...]).astype(BF16)
    st = st_sc[...]
    y_ref[...] = _dot(a, v) + _dot(qd, st.astype(BF16))
    kd = (k.astype(F32) * kd_sc[...]).astype(BF16)
    st_sc[...] = st * cd_sc[...] + _dot_tn(kd, v)


def _ret_bwd_kernel(lgb_ref, q_ref, k_ref, v_ref, g_ref, yp_ref, o_ref, st_sc, qd_sc, kd_sc, cd_sc, *, C):
    h = pl.program_id(1)
    c = pl.program_id(2)
    lgb = lgb_ref[h]

    @pl.when(c == 0)
    def _():
        st_sc[...] = jnp.zeros(st_sc.shape, F32)
        cd_sc[...] = jnp.exp(jnp.full(cd_sc.shape, C, F32) * lgb)
        idx = lax.broadcasted_iota(jnp.int32, (C, C_QK_DIM), 0).astype(F32)
        qd_sc[...] = jnp.exp((C - idx) * lgb)
        kd_sc[...] = jnp.exp(idx * lgb)

    q = q_ref[...]
    k = k_ref[...]
    v = v_ref[...]
    qd = (q.astype(F32) * qd_sc[...]).astype(BF16)
    st = st_sc[...]
    y = yp_ref[...] + _dot(qd, st.astype(BF16))
    kd = (k.astype(F32) * kd_sc[...]).astype(BF16)
    st_sc[...] = st * cd_sc[...] + _dot_tn(kd, v)
    mu = jnp.mean(y, axis=-1, keepdims=True)
    yc = y - mu
    var = jnp.mean(yc * yc, axis=-1, keepdims=True)
    yn = yc * lax.rsqrt(var + GN_EPS)
    g = g_ref[...].astype(F32)
    o_ref[...] = (g * jax.nn.sigmoid(g) * yn).astype(BF16)


def _retention(p, lgf, lgb, *, batch, seq, C):
    T = p.shape[0]
    nc = seq // C
    H = C_HEADS
    fwd = lambda b, h, c, *_: b * nc + c
    bwd = lambda b, h, c, *_: b * nc + (nc - 1 - c)
    y_part = pl.pallas_call(
        functools.partial(_ret_fwd_kernel, C=C),
        out_shape=jax.ShapeDtypeStruct((T, H * C_V_DIM), F32),
        grid_spec=pltpu.PrefetchScalarGridSpec(
            num_scalar_prefetch=2,
            grid=(batch, H, nc),
            in_specs=[
                pl.BlockSpec((C, C_QK_DIM), lambda b, h, c, *_: (fwd(b, h, c), h)),
                pl.BlockSpec((C, C_QK_DIM), lambda b, h, c, *_: (fwd(b, h, c), H + h)),
                pl.BlockSpec((C, C_V_DIM), lambda b, h, c, *_: (fwd(b, h, c), H + h)),
            ],
            out_specs=pl.BlockSpec((C, C_V_DIM), lambda b, h, c, *_: (fwd(b, h, c), h)),
            scratch_shapes=[
                pltpu.VMEM((C_QK_DIM, C_V_DIM), F32),
                pltpu.VMEM((C, C), F32),
                pltpu.VMEM((C, C_QK_DIM), F32),
                pltpu.VMEM((C, C_QK_DIM), F32),
                pltpu.VMEM((1, C_V_DIM), F32),
            ],
        ),
        compiler_params=_cparams(("arbitrary",) * 3),
        name="ret_fwd",
    )(lgf, lgb, p, p, p)
    return pl.pallas_call(
        functools.partial(_ret_bwd_kernel, C=C),
        out_shape=jax.ShapeDtypeStruct((T, H * C_V_DIM), BF16),
        grid_spec=pltpu.PrefetchScalarGridSpec(
            num_scalar_prefetch=1,
            grid=(batch, H, nc),
            in_specs=[
                pl.BlockSpec((C, C_QK_DIM), lambda b, h, c, *_: (bwd(b, h, c), h)),
                pl.BlockSpec((C, C_QK_DIM), lambda b, h, c, *_: (bwd(b, h, c), H + h)),
                pl.BlockSpec((C, C_V_DIM), lambda b, h, c, *_: (bwd(b, h, c), H + h)),
                pl.BlockSpec((C, C_V_DIM), lambda b, h, c, *_: (bwd(b, h, c), 2 * H + h)),
                pl.BlockSpec((C, C_V_DIM), lambda b, h, c, *_: (bwd(b, h, c), h)),
            ],
            out_specs=pl.BlockSpec((C, C_V_DIM), lambda b, h, c, *_: (bwd(b, h, c), h)),
            scratch_shapes=[
                pltpu.VMEM((C_QK_DIM, C_V_DIM), F32),
                pltpu.VMEM((C, C_QK_DIM), F32),
                pltpu.VMEM((C, C_QK_DIM), F32),
                pltpu.VMEM((1, C_V_DIM), F32),
            ],
        ),
        compiler_params=_cparams(("arbitrary",) * 3),
        name="ret_bwd",
    )(lgb, p, p, p, p, y_part)


def _deinterleave(d):
    return np.concatenate([np.arange(0, d, 2), np.arange(1, d, 2)])


def _rope_tables(seq_max):
    t = np.arange(seq_max)
    row = (t // GRID_W).astype(np.float32)
    col = (t % GRID_W).astype(np.float32)

    def ang(d_rot):
        n = d_rot // 4
        inv = jnp.asarray(ROPE_THETA, F32) ** (-jnp.arange(n, dtype=F32) / n)
        return jnp.concatenate([jnp.asarray(row)[:, None] * inv, jnp.asarray(col)[:, None] * inv], axis=-1)

    a = ang(A_HEAD_DIM)
    ca = jnp.concatenate([jnp.cos(a), jnp.cos(a)], axis=-1)
    sa = jnp.concatenate([-jnp.sin(a), jnp.sin(a)], axis=-1)
    b = ang(B_ROPE_DIM)
    z = jnp.zeros_like(b)
    cb = jnp.concatenate([jnp.cos(b), z, jnp.cos(b), z], axis=-1)
    sb = jnp.concatenate([-jnp.sin(b), z, jnp.sin(b), z], axis=-1)
    c = ang(C_QK_DIM)
    return ca, sa, cb, sb, jnp.cos(c), jnp.sin(c)


def _prep_even(w_in, g_qa, g_ka, g_cq, w_uq, g_ckv, w_ukv, w_out):
    perm_a = _deinterleave(A_HEAD_DIM)
    perm_b = _deinterleave(B_ROPE_DIM)
    o = 0
    wqa = w_in[:, o:o + 1024].reshape(D_MODEL, A_HEADS, 128)[:, :, perm_a].reshape(D_MODEL, 1024)
    o += 1024
    wka = w_in[:, o:o + 256].reshape(D_MODEL, A_KV_HEADS, 128)[:, :, perm_a].reshape(D_MODEL, 256)
    o += 256
    wva = w_in[:, o:o + 256]
    o += 256
    wcq = w_in[:, o:o + 512]
    o += 512
    wckv = w_in[:, o:o + 256]
    o += 256
    wkr = w_in[:, o:o + 64][:, perm_b]
    z32 = jnp.zeros((D_MODEL, 32), w_in.dtype)
    wkr = jnp.concatenate([wkr[:, :32], z32, wkr[:, 32:], z32], axis=-1)
    w1 = jnp.concatenate([wqa, wka, wva, wcq, wckv, wkr], axis=-1).astype(BF16)
    uq = w_uq.reshape(B_Q_RANK, B_HEADS, B_NOPE_DIM + B_ROPE_DIM)
    uq_nope = uq[:, :, :B_NOPE_DIM].reshape(B_Q_RANK, 1024)
    uq_rope = uq[:, :, B_NOPE_DIM:][:, :, perm_b]
    zr = jnp.zeros((B_Q_RANK, B_HEADS, 32), w_uq.dtype)
    uq_rope = jnp.concatenate([uq_rope[:, :, :32], zr, uq_rope[:, :, 32:], zr], axis=-1).reshape(B_Q_RANK, 1024)
    wuq = jnp.concatenate([uq_nope, uq_rope], axis=-1).astype(BF16)
    ukv = w_ukv.reshape(B_KV_RANK, B_HEADS, B_NOPE_DIM + B_V_DIM)
    wukv = jnp.concatenate([ukv[:, :, :B_NOPE_DIM].reshape(B_KV_RANK, 1024),
                            ukv[:, :, B_NOPE_DIM:].reshape(B_KV_RANK, 1024)], axis=-1).astype(BF16)
    return dict(w1=w1, gqa=g_qa[perm_a][None], gka=g_ka[perm_a][None], gcq=g_cq[None], gckv=g_ckv[None],
                wuq=wuq, wukv=wukv, wout=w_out.astype(BF16))


def _prep_odd(w_in, decay_f, decay_b, w_out):
    perm_c = _deinterleave(C_QK_DIM)
    nqk = C_HEADS * C_QK_DIM
    wq = w_in[:, :nqk].reshape(D_MODEL, C_HEADS, C_QK_DIM)[:, :, perm_c].reshape(D_MODEL, nqk)
    wk = w_in[:, nqk:2 * nqk].reshape(D_MODEL, C_HEADS, C_QK_DIM)[:, :, perm_c].reshape(D_MODEL, nqk)
    w = jnp.concatenate([wq, wk, w_in[:, 2 * nqk:]], axis=-1).astype(BF16)
    return dict(w=w, lgf=jax.nn.log_sigmoid(decay_f.astype(F32)), lgb=jax.nn.log_sigmoid(decay_b.astype(F32)),
                wout=w_out.astype(BF16))


def _trunk(x, *, batch, seq, tables, norms, even, odd, wup, wdn):
    ca, sa, cb, sb, cc, sc = tables
    g_mix_pre, g_mix_post, g_mlp_pre, g_mlp_post = norms
    for layer in range(DEPTH):
        if layer % 2 == 0:
            e = even[layer // 2]
            qa, ka, va, qbn, qbr, kbn, vb, kr = _even_proj(
                x, g_mix_pre[layer][None], e["w1"], e["gqa"], e["gka"], e["gcq"], e["gckv"], e["wuq"], e["wukv"],
                ca, sa, cb, sb, seq=seq, tm=512)
            tk = min(seq, 2048)
            oa = _attention(qa, ka, va, batch=batch, seq=seq, n_kv=A_KV_HEADS, G=A_HEADS // A_KV_HEADS,
                            tq=256, tk=tk)
            ob = _attention(qbn, kbn, vb, batch=batch, seq=seq, n_kv=B_HEADS, G=1, tq=512, tk=tk, qr=qbr, kr=kr)
            x = _out_proj(x, g_mix_post[layer][None], e["wout"], [(oa, 0), (ob, 0)], tm=512, tk=1024)
        else:
            o = odd[layer // 2]
            p = _odd_proj(x, g_mix_pre[layer][None], o["w"], cc, sc, seq=seq, tm=512)
            yg = _retention(p, o["lgf"], o["lgb"], batch=batch, seq=seq, C=256)
            x = _out_proj(x, g_mix_post[layer][None], o["wout"], [(yg, 0), (yg, 1)], tm=512, tk=2048)
        x = _mlp(x, g_mlp_pre[layer][None], g_mlp_post[layer][None], wup[layer], wdn[layer], tm=512, tf=512)
    return x


def kernel(x_prompt, x_sample, g_mix_pre, g_mix_post, g_mlp_pre, g_mlp_post, w_in_even, g_qnorm_a, g_knorm_a,
           g_cq_b, w_uq_b, g_ckv_b, w_ukv_b, w_out_even, w_in_odd, decay_fwd, decay_bwd, w_out_odd,
           w_mlp_up, w_mlp_down):
    n_even, n_odd = w_in_even.shape[0], w_in_odd.shape[0]
    even = [_prep_even(w_in_even[e], g_qnorm_a[e], g_knorm_a[e], g_cq_b[e], w_uq_b[e], g_ckv_b[e], w_ukv_b[e],
                       w_out_even[e]) for e in range(n_even)]
    odd = [_prep_odd(w_in_odd[o], decay_fwd[o], decay_bwd[o], w_out_odd[o]) for o in range(n_odd)]
    wup = w_mlp_up.astype(BF16)
    wdn = w_mlp_down.astype(BF16)
    norms = (g_mix_pre, g_mix_post, g_mlp_pre, g_mlp_post)
    tables = _rope_tables(max(x_prompt.shape[1], x_sample.shape[1]))
    outs = []
    for xg in (x_prompt, x_sample):
        b, s, d = xg.shape
        y = _trunk(xg.reshape(b * s, d), batch=b, seq=s, tables=tables, norms=norms, even=even, odd=odd,
                   wup=wup, wdn=wdn)
        outs.append(y.reshape(b, s, d))
    return tuple(outs)
```

```python
import functools

import jax
import jax.numpy as jnp
import numpy as np
from jax import lax
from jax.experimental import pallas as pl
from jax.experimental.pallas import tpu as pltpu

D_MODEL = 2048
DEPTH = 4
GRID_W = 64
ROPE_THETA = 10000.0
NORM_EPS = 1e-6
GN_EPS = 1e-5
A_HEADS, A_KV_HEADS, A_HEAD_DIM = 8, 2, 128
B_HEADS, B_Q_RANK, B_KV_RANK, B_NOPE_DIM, B_ROPE_DIM, B_V_DIM = 8, 512, 256, 128, 64, 128
C_HEADS, C_QK_DIM, C_V_DIM = 8, 256, 512
D_FF = 4 * D_MODEL

VMEM_LIMIT_BYTES = 56 * 1024 * 1024

BF16 = jnp.bfloat16
F32 = jnp.float32


def _cparams(semantics):
    return pltpu.CompilerParams(dimension_semantics=semantics, vmem_limit_bytes=VMEM_LIMIT_BYTES)


def _rms(x, g):
    return x * lax.rsqrt(jnp.mean(x * x, axis=-1, keepdims=True) + NORM_EPS) * g


def _dot(a, b):
    return jnp.dot(a, b, preferred_element_type=F32)


def _dot_nt(a, b):
    return lax.dot_general(a, b, (((1,), (1,)), ((), ())), preferred_element_type=F32)


def _dot_tn(a, b):
    return lax.dot_general(a, b, (((0,), (0,)), ((), ())), preferred_element_type=F32)


def _rope_half(x, c, s):
    return x * c + pltpu.roll(x, 64, 1) * s


EVEN_W1_COLS = 1024 + 256 + 256 + 512 + 256 + 128


def _even_proj_kernel(x_ref, gpre_ref, w1_ref, gqa_ref, gka_ref, gcq_ref, gckv_ref, wuq_ref, wukv_ref,
                      ca_ref, sa_ref, cb_ref, sb_ref,
                      qa_ref, ka_ref, va_ref, qbn_ref, qbr_ref, kbn_ref, vb_ref, kr_ref):
    hn = _rms(x_ref[...], gpre_ref[...]).astype(BF16)
    p = _dot(hn, w1_ref[...])
    ca, sa = ca_ref[...], sa_ref[...]
    cb, sb = cb_ref[...], sb_ref[...]
    gq = gqa_ref[...] * (A_HEAD_DIM ** -0.5)
    gk = gka_ref[...]
    for h in range(A_HEADS):
        xh = p[:, h * 128:(h + 1) * 128]
        qa_ref[:, h * 128:(h + 1) * 128] = _rope_half(_rms(xh, gq), ca, sa).astype(BF16)
    for h in range(A_KV_HEADS):
        xh = p[:, 1024 + h * 128:1024 + (h + 1) * 128]
        ka_ref[:, h * 128:(h + 1) * 128] = _rope_half(_rms(xh, gk), ca, sa).astype(BF16)
    va_ref[...] = p[:, 1280:1536].astype(BF16)
    cqn = _rms(p[:, 1536:2048], gcq_ref[...]).astype(BF16)
    qb = _dot(cqn, wuq_ref[...]) * ((B_NOPE_DIM + B_ROPE_DIM) ** -0.5)
    qbn_ref[...] = qb[:, :1024].astype(BF16)
    for h in range(B_HEADS):
        xh = qb[:, 1024 + h * 128:1024 + (h + 1) * 128]
        qbr_ref[:, h * 128:(h + 1) * 128] = _rope_half(xh, cb, sb).astype(BF16)
    ckvn = _rms(p[:, 2048:2304], gckv_ref[...]).astype(BF16)
    kv = _dot(ckvn, wukv_ref[...])
    kbn_ref[...] = kv[:, :1024].astype(BF16)
    vb_ref[...] = kv[:, 1024:].astype(BF16)
    kr_ref[...] = _rope_half(p[:, 2304:2432], cb, sb).astype(BF16)


def _even_proj(x, gpre, w1, gqa, gka, gcq, gckv, wuq, wukv, ca, sa, cb, sb, *, seq, tm):
    T = x.shape[0]
    nblk_seq = seq // tm
    row = lambda i: (i, 0)
    fixed = lambda i: (0, 0)
    pos = lambda i: (i % nblk_seq, 0)
    out_cols = (1024, 256, 256, 1024, 1024, 1024, 1024, 128)
    return pl.pallas_call(
        _even_proj_kernel,
        out_shape=[jax.ShapeDtypeStruct((T, c), BF16) for c in out_cols],
        grid=(T // tm,),
        in_specs=[
            pl.BlockSpec((tm, D_MODEL), row),
            pl.BlockSpec((1, D_MODEL), fixed),
            pl.BlockSpec((D_MODEL, EVEN_W1_COLS), fixed, pipeline_mode=pl.Buffered(1)),
            pl.BlockSpec((1, 128), fixed),
            pl.BlockSpec((1, 128), fixed),
            pl.BlockSpec((1, B_Q_RANK), fixed),
            pl.BlockSpec((1, B_KV_RANK), fixed),
            pl.BlockSpec((B_Q_RANK, 2048), fixed, pipeline_mode=pl.Buffered(1)),
            pl.BlockSpec((B_KV_RANK, 2048), fixed, pipeline_mode=pl.Buffered(1)),
            pl.BlockSpec((tm, 128), pos),
            pl.BlockSpec((tm, 128), pos),
            pl.BlockSpec((tm, 128), pos),
            pl.BlockSpec((tm, 128), pos),
        ],
        out_specs=[pl.BlockSpec((tm, c), row) for c in out_cols],
        compiler_params=_cparams(("arbitrary",)),
        name="even_proj",
    )(x, gpre, w1, gqa, gka, gcq, gckv, wuq, wukv, ca, sa, cb, sb)


def _attn_kernel(*refs, G, rope_ext, shared_kv, nk):
    if rope_ext:
        q_ref, qr_ref, k_ref, kr_ref, v_ref, o_ref = refs[:6]
        scratch = refs[6:]
    else:
        q_ref, k_ref, v_ref, o_ref = refs[:4]
        qr_ref = kr_ref = None
        scratch = refs[4:]
    ki = pl.program_id(3)

    def head(g):
        sl = slice(g * 128, (g + 1) * 128)
        kv_sl = slice(0, 128) if shared_kv else sl
        q, k, v = q_ref[:, sl], k_ref[:, kv_sl], v_ref[:, kv_sl]
        if rope_ext:
            q = jnp.concatenate([q, qr_ref[:, sl]], axis=-1)
            k = jnp.concatenate([k, kr_ref[...]], axis=-1)
        return _dot_nt(q, k), v

    if nk == 1:
        for g in range(G):
            s, v = head(g)
            m = jnp.max(s, axis=-1, keepdims=True)
            p = jnp.exp(s - m)
            l = jnp.sum(p, axis=-1, keepdims=True)
            o = _dot(p.astype(BF16), v)
            o_ref[:, g * 128:(g + 1) * 128] = (o / l).astype(o_ref.dtype)
        return

    m_sc, l_sc, acc_sc = scratch

    @pl.when(ki == 0)
    def _():
        m_sc[...] = jnp.full(m_sc.shape, -jnp.inf, F32)
        l_sc[...] = jnp.zeros(l_sc.shape, F32)
        acc_sc[...] = jnp.zeros(acc_sc.shape, F32)

    for g in range(G):
        s, v = head(g)
        m_prev = m_sc[g]
        m_new = jnp.maximum(m_prev, jnp.max(s, axis=-1, keepdims=True))
        alpha = jnp.exp(m_prev - m_new)
        p = jnp.exp(s - m_new)
        l_sc[g] = alpha * l_sc[g] + jnp.sum(p, axis=-1, keepdims=True)
        acc_sc[g] = alpha * acc_sc[g] + _dot(p.astype(BF16), v)
        m_sc[g] = m_new

    @pl.when(ki == nk - 1)
    def _():
        for g in range(G):
            o_ref[:, g * 128:(g + 1) * 128] = (acc_sc[g] / l_sc[g]).astype(o_ref.dtype)


def _attention(q, k, v, *, batch, seq, n_groups, G, shared_kv, tq, tk, qr=None, kr=None):
    T = q.shape[0]
    nq, nk = seq // tq, seq // tk
    rope_ext = qr is not None
    kvw = 128 if shared_kv else G * 128
    q_map = lambda b, h, qi, ki: (b * nq + qi, h)
    kv_map = lambda b, h, qi, ki: (b * nk + ki, h)
    kr_map = lambda b, h, qi, ki: (b * nk + ki, 0)
    in_specs = [pl.BlockSpec((tq, G * 128), q_map)]
    args = [q]
    if rope_ext:
        in_specs.append(pl.BlockSpec((tq, G * 128), q_map))
        args.append(qr)
    in_specs.append(pl.BlockSpec((tk, kvw), kv_map))
    args.append(k)
    if rope_ext:
        in_specs.append(pl.BlockSpec((tk, 128), kr_map))
        args.append(kr)
    in_specs.append(pl.BlockSpec((tk, kvw), kv_map))
    args.append(v)
    scratch = []
    if nk > 1:
        scratch = [pltpu.VMEM((G, tq, 1), F32), pltpu.VMEM((G, tq, 1), F32), pltpu.VMEM((G, tq, 128), F32)]
    return pl.pallas_call(
        functools.partial(_attn_kernel, G=G, rope_ext=rope_ext, shared_kv=shared_kv, nk=nk),
        out_shape=jax.ShapeDtypeStruct((T, n_groups * G * 128), BF16),
        grid=(batch, n_groups, nq, nk),
        in_specs=in_specs,
        out_specs=pl.BlockSpec((tq, G * 128), q_map),
        scratch_shapes=scratch,
        compiler_params=_cparams(("arbitrary",) * 4),
        name="attn_mla" if rope_ext else "attn_gqa",
    )(*args)


def _out_proj_kernel(*refs, widths):
    x_ref, g_ref, w_ref = refs[:3]
    a_refs = refs[3:3 + len(widths)]
    o_ref = refs[3 + len(widths)]
    acc = None
    off = 0
    for a_ref, wd in zip(a_refs, widths):
        part = _dot(a_ref[...], w_ref[off:off + wd, :])
        acc = part if acc is None else acc + part
        off += wd
    o_ref[...] = x_ref[...] + _rms(acc, g_ref[...])


def _out_proj(x, g, w, parts, *, tm):
    T = x.shape[0]
    widths = tuple(wd for _, wd, _ in parts)
    row = lambda i: (i, 0)
    return pl.pallas_call(
        functools.partial(_out_proj_kernel, widths=widths),
        out_shape=jax.ShapeDtypeStruct((T, D_MODEL), F32),
        grid=(T // tm,),
        in_specs=[
            pl.BlockSpec((tm, D_MODEL), row),
            pl.BlockSpec((1, D_MODEL), lambda i: (0, 0)),
            pl.BlockSpec(w.shape, lambda i: (0, 0), pipeline_mode=pl.Buffered(1)),
        ] + [pl.BlockSpec((tm, wd), functools.partial(lambda i, cb: (i, cb), cb=cb)) for _, wd, cb in parts],
        out_specs=pl.BlockSpec((tm, D_MODEL), row),
        compiler_params=_cparams(("arbitrary",)),
        name="out_proj",
    )(x, g, w, *[a for a, _, _ in parts])


def _mlp_kernel(x_ref, gpre_ref, gpost_ref, wup_ref, wdn_ref, o_ref, hn_sc, *, nf):
    f = pl.program_id(1)

    @pl.when(f == 0)
    def _():
        hn_sc[...] = _rms(x_ref[...], gpre_ref[...]).astype(BF16)
        o_ref[...] = jnp.zeros(o_ref.shape, F32)

    h = jnp.maximum(_dot(hn_sc[...], wup_ref[...]), 0.0)
    o_ref[...] += _dot((h * h).astype(BF16), wdn_ref[...])

    @pl.when(f == nf - 1)
    def _():
        o_ref[...] = x_ref[...] + _rms(o_ref[...], gpost_ref[...])


def _mlp(x, gpre, gpost, wup, wdn, *, tm, tf):
    T = x.shape[0]
    nf = D_FF // tf
    row = lambda i, f: (i, 0)
    fixed = lambda i, f: (0, 0)
    return pl.pallas_call(
        functools.partial(_mlp_kernel, nf=nf),
        out_shape=jax.ShapeDtypeStruct((T, D_MODEL), F32),
        grid=(T // tm, nf),
        in_specs=[
            pl.BlockSpec((tm, D_MODEL), row),
            pl.BlockSpec((1, D_MODEL), fixed),
            pl.BlockSpec((1, D_MODEL), fixed),
            pl.BlockSpec((D_MODEL, tf), lambda i, f: (0, f)),
            pl.BlockSpec((tf, D_MODEL), lambda i, f: (f, 0)),
        ],
        out_specs=pl.BlockSpec((tm, D_MODEL), row),
        scratch_shapes=[pltpu.VMEM((tm, D_MODEL), BF16)],
        compiler_params=_cparams(("arbitrary", "arbitrary")),
        name="mlp",
    )(x, gpre, gpost, wup, wdn)


ODD_TN = 1024


def _odd_proj_kernel(x_ref, gpre_ref, w_ref, *rest, rope, n_q_blocks):
    if rope:
        c_ref, s_ref, o_ref, hn_sc = rest
    else:
        o_ref, hn_sc = rest
    j = pl.program_id(1)

    @pl.when(j == 0)
    def _():
        hn_sc[...] = _rms(x_ref[...], gpre_ref[...]).astype(BF16)

    acc = _dot(hn_sc[...], w_ref[...])
    if not rope:
        o_ref[...] = acc.astype(BF16)
        return
    scale = jnp.where(j < n_q_blocks, 1.0, C_QK_DIM ** -0.5).astype(F32)
    c = c_ref[...] * scale
    s = s_ref[...] * scale
    for h in range(ODD_TN // C_QK_DIM):
        x0 = acc[:, h * 256:h * 256 + 128]
        x1 = acc[:, h * 256 + 128:(h + 1) * 256]
        o_ref[:, h * 256:h * 256 + 128] = (x0 * c - x1 * s).astype(BF16)
        o_ref[:, h * 256 + 128:(h + 1) * 256] = (x0 * s + x1 * c).astype(BF16)


def _odd_proj(x, gpre, w, *, tm, seq=None, tables=None):
    T = x.shape[0]
    N = w.shape[1]
    rope = tables is not None
    in_specs = [
        pl.BlockSpec((tm, D_MODEL), lambda i, j: (i, 0)),
        pl.BlockSpec((1, D_MODEL), lambda i, j: (0, 0)),
        pl.BlockSpec((D_MODEL, ODD_TN), lambda i, j: (0, j)),
    ]
    args = [x, gpre, w]
    if rope:
        nblk_seq = seq // tm
        in_specs += [pl.BlockSpec((tm, 128), lambda i, j: (i % nblk_seq, 0))] * 2
        args += list(tables)
    return pl.pallas_call(
        functools.partial(_odd_proj_kernel, rope=rope, n_q_blocks=N // ODD_TN // 2),
        out_shape=jax.ShapeDtypeStruct((T, N), BF16),
        grid=(T // tm, N // ODD_TN),
        in_specs=in_specs,
        out_specs=pl.BlockSpec((tm, ODD_TN), lambda i, j: (i, j)),
        scratch_shapes=[pltpu.VMEM((tm, D_MODEL), BF16)],
        compiler_params=_cparams(("arbitrary", "arbitrary")),
        name="odd_qk_proj" if rope else "odd_vg_proj",
    )(*args)


def _ret_fwd_kernel(lgf_ref, lgb_ref, q_ref, k_ref, v_ref, y_ref, st_sc, d_sc, qd_sc, kd_sc, cd_sc, *, C, n_chunks):
    h = pl.program_id(1)
    lgf = lgf_ref[h]
    lgb = lgb_ref[h]

    @pl.when(pl.program_id(2) == 0)
    def _():
        st_sc[...] = jnp.zeros(st_sc.shape, F32)
        cd_sc[...] = jnp.exp(jnp.full(cd_sc.shape, C, F32) * lgf)
        i = lax.broadcasted_iota(jnp.int32, (C, C), 0)
        j = lax.broadcasted_iota(jnp.int32, (C, C), 1)
        rel = (i - j).astype(F32)
        d_sc[...] = jnp.where(rel >= 0, jnp.exp(rel * lgf), jnp.exp(-rel * lgb))
        idx = lax.broadcasted_iota(jnp.int32, (C, C_QK_DIM), 0).astype(F32)
        qd_sc[...] = jnp.exp((idx + 1.0) * lgf)
        kd_sc[...] = jnp.exp((C - 1.0 - idx) * lgf)

    st = st_sc[...]
    for c in range(n_chunks):
        rows = slice(c * C, (c + 1) * C)
        q, k, v = q_ref[rows, :], k_ref[rows, :], v_ref[rows, :]
        a = (_dot_nt(q, k) * d_sc[...]).astype(BF16)
        qd = (q.astype(F32) * qd_sc[...]).astype(BF16)
        y_ref[rows, :] = _dot(a, v) + _dot(qd, st.astype(BF16))
        kd = (k.astype(F32) * kd_sc[...]).astype(BF16)
        st = st * cd_sc[...] + _dot_tn(kd, v)
    st_sc[...] = st


def _ret_bwd_kernel(lgb_ref, q_ref, k_ref, v_ref, g_ref, yp_ref, o_ref, st_sc, qd_sc, kd_sc, cd_sc, *, C, n_chunks):
    h = pl.program_id(1)
    lgb = lgb_ref[h]

    @pl.when(pl.program_id(2) == 0)
    def _():
        st_sc[...] = jnp.zeros(st_sc.shape, F32)
        cd_sc[...] = jnp.exp(jnp.full(cd_sc.shape, C, F32) * lgb)
        idx = lax.broadcasted_iota(jnp.int32, (C, C_QK_DIM), 0).astype(F32)
        qd_sc[...] = jnp.exp((C - idx) * lgb)
        kd_sc[...] = jnp.exp(idx * lgb)

    st = st_sc[...]
    for c in reversed(range(n_chunks)):
        rows = slice(c * C, (c + 1) * C)
        q, k, v = q_ref[rows, :], k_ref[rows, :], v_ref[rows, :]
        qd = (q.astype(F32) * qd_sc[...]).astype(BF16)
        y = yp_ref[rows, :] + _dot(qd, st.astype(BF16))
        kd = (k.astype(F32) * kd_sc[...]).astype(BF16)
        st = st * cd_sc[...] + _dot_tn(kd, v)
        mu = jnp.mean(y, axis=-1, keepdims=True)
        yc = y - mu
        var = jnp.mean(yc * yc, axis=-1, keepdims=True)
        yn = yc * lax.rsqrt(var + GN_EPS)
        g = g_ref[rows, :].astype(F32)
        o_ref[rows, :] = (g * jax.nn.sigmoid(g) * yn).astype(BF16)
    st_sc[...] = st


def _retention(qk, vg, lgf, lgb, *, batch, seq, C, rows):
    T = qk.shape[0]
    nb = seq // rows
    n_chunks = rows // C
    H = C_HEADS
    fwd = lambda b, h, r: b * nb + r
    bwd = lambda b, h, r: b * nb + (nb - 1 - r)
    qk_blk = (rows, C_QK_DIM)
    v_blk = (rows, C_V_DIM)
    state_scratch = [pltpu.VMEM((C_QK_DIM, C_V_DIM), F32)]
    dec_scratch = [pltpu.VMEM((C, C_QK_DIM), F32), pltpu.VMEM((C, C_QK_DIM), F32), pltpu.VMEM((1, C_V_DIM), F32)]
    y_part = pl.pallas_call(
        functools.partial(_ret_fwd_kernel, C=C, n_chunks=n_chunks),
        out_shape=jax.ShapeDtypeStruct((T, H * C_V_DIM), F32),
        grid_spec=pltpu.PrefetchScalarGridSpec(
            num_scalar_prefetch=2,
            grid=(batch, H, nb),
            in_specs=[
                pl.BlockSpec(qk_blk, lambda b, h, r, *_: (fwd(b, h, r), h)),
                pl.BlockSpec(qk_blk, lambda b, h, r, *_: (fwd(b, h, r), H + h)),
                pl.BlockSpec(v_blk, lambda b, h, r, *_: (fwd(b, h, r), h)),
            ],
            out_specs=pl.BlockSpec(v_blk, lambda b, h, r, *_: (fwd(b, h, r), h)),
            scratch_shapes=state_scratch + [pltpu.VMEM((C, C), F32)] + dec_scratch,
        ),
        compiler_params=_cparams(("arbitrary",) * 3),
        name="ret_fwd",
    )(lgf, lgb, qk, qk, vg)
    return pl.pallas_call(
        functools.partial(_ret_bwd_kernel, C=C, n_chunks=n_chunks),
        out_shape=jax.ShapeDtypeStruct((T, H * C_V_DIM), BF16),
        grid_spec=pltpu.PrefetchScalarGridSpec(
            num_scalar_prefetch=1,
            grid=(batch, H, nb),
            in_specs=[
                pl.BlockSpec(qk_blk, lambda b, h, r, *_: (bwd(b, h, r), h)),
                pl.BlockSpec(qk_blk, lambda b, h, r, *_: (bwd(b, h, r), H + h)),
                pl.BlockSpec(v_blk, lambda b, h, r, *_: (bwd(b, h, r), h)),
                pl.BlockSpec(v_blk, lambda b, h, r, *_: (bwd(b, h, r), H + h)),
                pl.BlockSpec(v_blk, lambda b, h, r, *_: (bwd(b, h, r), h)),
            ],
            out_specs=pl.BlockSpec(v_blk, lambda b, h, r, *_: (bwd(b, h, r), h)),
            scratch_shapes=state_scratch + dec_scratch,
        ),
        compiler_params=_cparams(("arbitrary",) * 3),
        name="ret_bwd",
    )(lgb, qk, qk, vg, vg, y_part)


def _deinterleave(d):
    return np.concatenate([np.arange(0, d, 2), np.arange(1, d, 2)])


def _rope_tables(seq_max):
    t = np.arange(seq_max)
    row = (t // GRID_W).astype(np.float32)
    col = (t % GRID_W).astype(np.float32)

    def ang(d_rot):
        n = d_rot // 4
        inv = jnp.asarray(ROPE_THETA, F32) ** (-jnp.arange(n, dtype=F32) / n)
        return jnp.concatenate([jnp.asarray(row)[:, None] * inv, jnp.asarray(col)[:, None] * inv], axis=-1)

    a = ang(A_HEAD_DIM)
    ca = jnp.concatenate([jnp.cos(a), jnp.cos(a)], axis=-1)
    sa = jnp.concatenate([-jnp.sin(a), jnp.sin(a)], axis=-1)
    b = ang(B_ROPE_DIM)
    z = jnp.zeros_like(b)
    cb = jnp.concatenate([jnp.cos(b), z, jnp.cos(b), z], axis=-1)
    sb = jnp.concatenate([-jnp.sin(b), z, jnp.sin(b), z], axis=-1)
    c = ang(C_QK_DIM)
    return ca, sa, cb, sb, jnp.cos(c), jnp.sin(c)


def _prep_even(w_in, g_qa, g_ka, g_cq, w_uq, g_ckv, w_ukv, w_out):
    perm_a = _deinterleave(A_HEAD_DIM)
    perm_b = _deinterleave(B_ROPE_DIM)
    o = 0
    wqa = w_in[:, o:o + 1024].reshape(D_MODEL, A_HEADS, 128)[:, :, perm_a].reshape(D_MODEL, 1024)
    o += 1024
    wka = w_in[:, o:o + 256].reshape(D_MODEL, A_KV_HEADS, 128)[:, :, perm_a].reshape(D_MODEL, 256)
    o += 256
    wva = w_in[:, o:o + 256]
    o += 256
    wcq = w_in[:, o:o + 512]
    o += 512
    wckv = w_in[:, o:o + 256]
    o += 256
    wkr = w_in[:, o:o + 64][:, perm_b]
    z32 = jnp.zeros((D_MODEL, 32), w_in.dtype)
    wkr = jnp.concatenate([wkr[:, :32], z32, wkr[:, 32:], z32], axis=-1)
    w1 = jnp.concatenate([wqa, wka, wva, wcq, wckv, wkr], axis=-1).astype(BF16)
    uq = w_uq.reshape(B_Q_RANK, B_HEADS, B_NOPE_DIM + B_ROPE_DIM)
    uq_nope = uq[:, :, :B_NOPE_DIM].reshape(B_Q_RANK, 1024)
    uq_rope = uq[:, :, B_NOPE_DIM:][:, :, perm_b]
    zr = jnp.zeros((B_Q_RANK, B_HEADS, 32), w_uq.dtype)
    uq_rope = jnp.concatenate([uq_rope[:, :, :32], zr, uq_rope[:, :, 32:], zr], axis=-1).reshape(B_Q_RANK, 1024)
    wuq = jnp.concatenate([uq_nope, uq_rope], axis=-1).astype(BF16)
    ukv = w_ukv.reshape(B_KV_RANK, B_HEADS, B_NOPE_DIM + B_V_DIM)
    wukv = jnp.concatenate([ukv[:, :, :B_NOPE_DIM].reshape(B_KV_RANK, 1024),
                            ukv[:, :, B_NOPE_DIM:].reshape(B_KV_RANK, 1024)], axis=-1).astype(BF16)
    return dict(w1=w1, gqa=g_qa[perm_a][None], gka=g_ka[perm_a][None], gcq=g_cq[None], gckv=g_ckv[None],
                wuq=wuq, wukv=wukv, wout=w_out.astype(BF16))


def _prep_odd(w_in, decay_f, decay_b, w_out):
    perm_c = _deinterleave(C_QK_DIM)
    nqk = C_HEADS * C_QK_DIM
    wq = w_in[:, :nqk].reshape(D_MODEL, C_HEADS, C_QK_DIM)[:, :, perm_c].reshape(D_MODEL, nqk)
    wk = w_in[:, nqk:2 * nqk].reshape(D_MODEL, C_HEADS, C_QK_DIM)[:, :, perm_c].reshape(D_MODEL, nqk)
    return dict(wqk=jnp.concatenate([wq, wk], axis=-1).astype(BF16), wvg=w_in[:, 2 * nqk:].astype(BF16),
                lgf=jax.nn.log_sigmoid(decay_f.astype(F32)), lgb=jax.nn.log_sigmoid(decay_b.astype(F32)),
                wout=w_out.astype(BF16))


def _trunk(x, *, batch, seq, tables, norms, even, odd, wup, wdn):
    ca, sa, cb, sb, cc, sc = tables
    g_mix_pre, g_mix_post, g_mlp_pre, g_mlp_post = norms
    for layer in range(DEPTH):
        gpre = g_mix_pre[layer][None]
        gpost = g_mix_post[layer][None]
        if layer % 2 == 0:
            e = even[layer // 2]
            qa, ka, va, qbn, qbr, kbn, vb, kr = _even_proj(
                x, gpre, e["w1"], e["gqa"], e["gka"], e["gcq"], e["gckv"], e["wuq"], e["wukv"],
                ca, sa, cb, sb, seq=seq, tm=512)
            tk = min(seq, 2048)
            oa = _attention(qa, ka, va, batch=batch, seq=seq, n_groups=A_KV_HEADS, G=A_HEADS // A_KV_HEADS,
                            shared_kv=True, tq=512, tk=tk)
            ob = _attention(qbn, kbn, vb, batch=batch, seq=seq, n_groups=2, G=B_HEADS // 2, shared_kv=False,
                            tq=512, tk=tk, qr=qbr, kr=kr)
            x = _out_proj(x, gpost, e["wout"], [(oa, 1024, 0), (ob, 1024, 0)], tm=512)
        else:
            o = odd[layer // 2]
            qk = _odd_proj(x, gpre, o["wqk"], tm=1024, seq=seq, tables=(cc, sc))
            vg = _odd_proj(x, gpre, o["wvg"], tm=1024)
            yg = _retention(qk, vg, o["lgf"], o["lgb"], batch=batch, seq=seq, C=256, rows=2048)
            x = _out_proj(x, gpost, o["wout"], [(yg, 4096, 0)], tm=512)
        x = _mlp(x, g_mlp_pre[layer][None], g_mlp_post[layer][None], wup[layer], wdn[layer], tm=1024, tf=512)
    return x


def kernel(x_prompt, x_sample, g_mix_pre, g_mix_post, g_mlp_pre, g_mlp_post, w_in_even, g_qnorm_a, g_knorm_a,
           g_cq_b, w_uq_b, g_ckv_b, w_ukv_b, w_out_even, w_in_odd, decay_fwd, decay_bwd, w_out_odd,
           w_mlp_up, w_mlp_down):
    n_even, n_odd = w_in_even.shape[0], w_in_odd.shape[0]
    even = [_prep_even(w_in_even[e], g_qnorm_a[e], g_knorm_a[e], g_cq_b[e], w_uq_b[e], g_ckv_b[e], w_ukv_b[e],
                       w_out_even[e]) for e in range(n_even)]
    odd = [_prep_odd(w_in_odd[o], decay_fwd[o], decay_bwd[o], w_out_odd[o]) for o in range(n_odd)]
    wup = w_mlp_up.astype(BF16)
    wdn = w_mlp_down.astype(BF16)
    norms = (g_mix_pre, g_mix_post, g_mlp_pre, g_mlp_post)
    tables = _rope_tables(max(x_prompt.shape[1], x_sample.shape[1]))
    outs = []
    for xg in (x_prompt, x_sample):
        b, s, d = xg.shape
        y = _trunk(xg.reshape(b * s, d), batch=b, seq=s, tables=tables, norms=norms, even=even, odd=odd,
                   wup=wup, wdn=wdn)
        outs.append(y.reshape(b, s, d))
    return tuple(outs)
```

```python
import functools

import jax
import jax.numpy as jnp
import numpy as np
from jax import lax
from jax.experimental import pallas as pl
from jax.experimental.pallas import tpu as pltpu

D_MODEL = 2048
DEPTH = 4
GRID_W = 64
ROPE_THETA = 10000.0
NORM_EPS = 1e-6
GN_EPS = 1e-5
A_HEADS, A_KV_HEADS, A_HEAD_DIM = 8, 2, 128
B_HEADS, B_Q_RANK, B_KV_RANK, B_NOPE_DIM, B_ROPE_DIM, B_V_DIM = 8, 512, 256, 128, 64, 128
C_HEADS, C_QK_DIM, C_V_DIM = 8, 256, 512
D_FF = 4 * D_MODEL

VMEM_LIMIT_BYTES = 56 * 1024 * 1024
ATTN_ROWS = 256
LOG2E = 1.4426950408889634

BF16 = jnp.bfloat16
F32 = jnp.float32


def _cparams(semantics):
    return pltpu.CompilerParams(dimension_semantics=semantics, vmem_limit_bytes=VMEM_LIMIT_BYTES)


def _rms(x, g):
    return x * lax.rsqrt(jnp.mean(x * x, axis=-1, keepdims=True) + NORM_EPS) * g


def _dot(a, b):
    return jnp.dot(a, b, preferred_element_type=F32)


def _dot_nt(a, b):
    return lax.dot_general(a, b, (((1,), (1,)), ((), ())), preferred_element_type=F32)


def _dot_tn(a, b):
    return lax.dot_general(a, b, (((0,), (0,)), ((), ())), preferred_element_type=F32)


def _rope_half(x, c, s):
    return x * c + pltpu.roll(x, 64, 1) * s


EVEN_W1_COLS = 1024 + 256 + 256 + 512 + 256 + 128


def _even_proj_kernel(x_ref, gpre_ref, w1_ref, gqa_ref, gka_ref, gcq_ref, gckv_ref, wuq_ref, wukv_ref,
                      ca_ref, sa_ref, cb_ref, sb_ref,
                      qa_ref, ka_ref, va_ref, qbn_ref, qbr_ref, kbn_ref, vb_ref, kr_ref):
    hn = _rms(x_ref[...], gpre_ref[...]).astype(BF16)
    p = _dot(hn, w1_ref[...])
    ca, sa = ca_ref[...], sa_ref[...]
    cb, sb = cb_ref[...], sb_ref[...]
    gq = gqa_ref[...] * (A_HEAD_DIM ** -0.5 * LOG2E)
    gk = gka_ref[...]
    for h in range(A_HEADS):
        xh = p[:, h * 128:(h + 1) * 128]
        qa_ref[:, h * 128:(h + 1) * 128] = _rope_half(_rms(xh, gq), ca, sa).astype(BF16)
    for h in range(A_KV_HEADS):
        xh = p[:, 1024 + h * 128:1024 + (h + 1) * 128]
        ka_ref[:, h * 128:(h + 1) * 128] = _rope_half(_rms(xh, gk), ca, sa).astype(BF16)
    va_ref[...] = p[:, 1280:1536].astype(BF16)
    cqn = _rms(p[:, 1536:2048], gcq_ref[...]).astype(BF16)
    qb = _dot(cqn, wuq_ref[...]) * ((B_NOPE_DIM + B_ROPE_DIM) ** -0.5 * LOG2E)
    qbn_ref[...] = qb[:, :1024].astype(BF16)
    for h in range(B_HEADS):
        xh = qb[:, 1024 + h * 128:1024 + (h + 1) * 128]
        qbr_ref[:, h * 128:(h + 1) * 128] = _rope_half(xh, cb, sb).astype(BF16)
    ckvn = _rms(p[:, 2048:2304], gckv_ref[...]).astype(BF16)
    kv = _dot(ckvn, wukv_ref[...])
    kbn_ref[...] = kv[:, :1024].astype(BF16)
    vb_ref[...] = kv[:, 1024:].astype(BF16)
    kr_ref[...] = _rope_half(p[:, 2304:2432], cb, sb).astype(BF16)


def _even_proj(x, gpre, w1, gqa, gka, gcq, gckv, wuq, wukv, ca, sa, cb, sb, *, seq, tm):
    T = x.shape[0]
    nblk_seq = seq // tm
    row = lambda i: (i, 0)
    fixed = lambda i: (0, 0)
    pos = lambda i: (i % nblk_seq, 0)
    out_cols = (1024, 256, 256, 1024, 1024, 1024, 1024, 128)
    return pl.pallas_call(
        _even_proj_kernel,
        out_shape=[jax.ShapeDtypeStruct((T, c), BF16) for c in out_cols],
        grid=(T // tm,),
        in_specs=[
            pl.BlockSpec((tm, D_MODEL), row),
            pl.BlockSpec((1, D_MODEL), fixed),
            pl.BlockSpec((D_MODEL, EVEN_W1_COLS), fixed, pipeline_mode=pl.Buffered(1)),
            pl.BlockSpec((1, 128), fixed),
            pl.BlockSpec((1, 128), fixed),
            pl.BlockSpec((1, B_Q_RANK), fixed),
            pl.BlockSpec((1, B_KV_RANK), fixed),
            pl.BlockSpec((B_Q_RANK, 2048), fixed, pipeline_mode=pl.Buffered(1)),
            pl.BlockSpec((B_KV_RANK, 2048), fixed, pipeline_mode=pl.Buffered(1)),
            pl.BlockSpec((tm, 128), pos),
            pl.BlockSpec((tm, 128), pos),
            pl.BlockSpec((tm, 128), pos),
            pl.BlockSpec((tm, 128), pos),
        ],
        out_specs=[pl.BlockSpec((tm, c), row) for c in out_cols],
        compiler_params=_cparams(("arbitrary",)),
        name="even_proj",
    )(x, gpre, w1, gqa, gka, gcq, gckv, wuq, wukv, ca, sa, cb, sb)


def _attn_kernel(*refs, G, heads_per_kv, rope_ext, nk):
    if rope_ext:
        q_ref, qr_ref, k_ref, kr_ref, v_ref, o_ref = refs[:6]
        scratch = refs[6:]
    else:
        q_ref, k_ref, v_ref, o_ref = refs[:4]
        qr_ref = kr_ref = None
        scratch = refs[4:]
    ki = pl.program_id(3)
    tq = q_ref.shape[0]
    items = [(g, slice(r, r + ATTN_ROWS)) for g in range(G) for r in range(0, tq, ATTN_ROWS)]

    def scores(g, rows):
        sl = slice(g * 128, (g + 1) * 128)
        kvh = g // heads_per_kv
        q, k = q_ref[rows, sl], k_ref[:, kvh * 128:(kvh + 1) * 128]
        if rope_ext:
            q = jnp.concatenate([q, qr_ref[rows, sl]], axis=-1)
            k = jnp.concatenate([k, kr_ref[...]], axis=-1)
        return _dot_nt(q, k)

    def weighted_values(g, p, v_pairs):
        kvh = g // heads_per_kv
        o = _dot(p.astype(BF16), v_pairs[kvh // 2])
        return o[:, (kvh % 2) * 128:(kvh % 2 + 1) * 128]

    n_pairs = v_ref.shape[1] // 256

    if nk == 1:
        v_pairs = [v_ref[:, i * 256:(i + 1) * 256] for i in range(n_pairs)]
        for g, rows in items:
            s = scores(g, rows)
            m = jnp.max(s, axis=-1, keepdims=True)
            p = jnp.exp2(s - m)
            l = jnp.sum(p, axis=-1, keepdims=True)
            o_ref[rows, g * 128:(g + 1) * 128] = (weighted_values(g, p, v_pairs) / l).astype(o_ref.dtype)
        return

    m_sc, l_sc, acc_sc = scratch

    @pl.when(ki == 0)
    def _():
        m_sc[...] = jnp.full(m_sc.shape, -jnp.inf, F32)
        l_sc[...] = jnp.zeros(l_sc.shape, F32)
        acc_sc[...] = jnp.zeros(acc_sc.shape, F32)

    neg_v_pairs = [-v_ref[:, i * 256:(i + 1) * 256] for i in range(n_pairs)]
    for g, rows in items:
        s = scores(g, rows)
        m_prev = m_sc[g, rows]
        m_new = jnp.maximum(m_prev, jnp.max(s, axis=-1, keepdims=True))
        alpha = jnp.exp2(m_prev - m_new)
        p = jnp.exp2(s - m_new)
        l_sc[g, rows] = alpha * l_sc[g, rows] + jnp.sum(p, axis=-1, keepdims=True)
        acc_sc[g, rows] = alpha * acc_sc[g, rows] - weighted_values(g, p, neg_v_pairs)
        m_sc[g, rows] = m_new

    @pl.when(ki == nk - 1)
    def _():
        for g in range(G):
            o_ref[:, g * 128:(g + 1) * 128] = (acc_sc[g] / l_sc[g]).astype(o_ref.dtype)


def _attention(q, k, v, *, batch, seq, G, heads_per_kv, tq, tk, qr=None, kr=None):
    T, qw = q.shape
    n_groups = qw // (G * 128)
    kvw = G // heads_per_kv * 128
    assert kvw % 256 == 0
    nq, nk = seq // tq, seq // tk
    rope_ext = qr is not None
    q_map = lambda b, h, qi, ki: (b * nq + qi, h)
    kv_map = lambda b, h, qi, ki: (b * nk + ki, h)
    kr_map = lambda b, h, qi, ki: (b * nk + ki, 0)
    in_specs = [pl.BlockSpec((tq, G * 128), q_map)]
    args = [q]
    if rope_ext:
        in_specs.append(pl.BlockSpec((tq, G * 128), q_map))
        args.append(qr)
    in_specs.append(pl.BlockSpec((tk, kvw), kv_map))
    args.append(k)
    if rope_ext:
        in_specs.append(pl.BlockSpec((tk, 128), kr_map))
        args.append(kr)
    in_specs.append(pl.BlockSpec((tk, kvw), kv_map))
    args.append(v)
    scratch = []
    if nk > 1:
        scratch = [pltpu.VMEM((G, tq, 1), F32), pltpu.VMEM((G, tq, 1), F32), pltpu.VMEM((G, tq, 128), F32)]
    return pl.pallas_call(
        functools.partial(_attn_kernel, G=G, heads_per_kv=heads_per_kv, rope_ext=rope_ext, nk=nk),
        out_shape=jax.ShapeDtypeStruct((T, qw), BF16),
        grid=(batch, n_groups, nq, nk),
        in_specs=in_specs,
        out_specs=pl.BlockSpec((tq, G * 128), q_map),
        scratch_shapes=scratch,
        compiler_params=_cparams(("arbitrary",) * 4),
        name="attn_mla" if rope_ext else "attn_gqa",
    )(*args)


def _out_proj_kernel(*refs, widths):
    x_ref, g_ref, w_ref = refs[:3]
    a_refs = refs[3:3 + len(widths)]
    o_ref = refs[3 + len(widths)]
    acc = None
    off = 0
    for a_ref, wd in zip(a_refs, widths):
        part = _dot(a_ref[...], w_ref[off:off + wd, :])
        acc = part if acc is None else acc + part
        off += wd
    o_ref[...] = x_ref[...] + _rms(acc, g_ref[...])


def _out_proj(x, g, w, parts, *, tm):
    T = x.shape[0]
    widths = tuple(wd for _, wd, _ in parts)
    row = lambda i: (i, 0)
    return pl.pallas_call(
        functools.partial(_out_proj_kernel, widths=widths),
        out_shape=jax.ShapeDtypeStruct((T, D_MODEL), F32),
        grid=(T // tm,),
        in_specs=[
            pl.BlockSpec((tm, D_MODEL), row),
            pl.BlockSpec((1, D_MODEL), lambda i: (0, 0)),
            pl.BlockSpec(w.shape, lambda i: (0, 0), pipeline_mode=pl.Buffered(1)),
        ] + [pl.BlockSpec((tm, wd), functools.partial(lambda i, cb: (i, cb), cb=cb)) for _, wd, cb in parts],
        out_specs=pl.BlockSpec((tm, D_MODEL), row),
        compiler_params=_cparams(("arbitrary",)),
        name="out_proj",
    )(x, g, w, *[a for a, _, _ in parts])


def _mlp_kernel(x_ref, gpre_ref, gpost_ref, wup_ref, wdn_ref, o_ref, hn_sc, *, nf):
    f = pl.program_id(1)

    @pl.when(f == 0)
    def _():
        hn_sc[...] = _rms(x_ref[...], gpre_ref[...]).astype(BF16)
        o_ref[...] = jnp.zeros(o_ref.shape, F32)

    h = jnp.maximum(_dot(hn_sc[...], wup_ref[...]), 0.0)
    o_ref[...] += _dot((h * h).astype(BF16), wdn_ref[...])

    @pl.when(f == nf - 1)
    def _():
        o_ref[...] = x_ref[...] + _rms(o_ref[...], gpost_ref[...])


def _mlp(x, gpre, gpost, wup, wdn, *, tm, tf):
    T = x.shape[0]
    nf = D_FF // tf
    row = lambda i, f: (i, 0)
    fixed = lambda i, f: (0, 0)
    return pl.pallas_call(
        functools.partial(_mlp_kernel, nf=nf),
        out_shape=jax.ShapeDtypeStruct((T, D_MODEL), F32),
        grid=(T // tm, nf),
        in_specs=[
            pl.BlockSpec((tm, D_MODEL), row),
            pl.BlockSpec((1, D_MODEL), fixed),
            pl.BlockSpec((1, D_MODEL), fixed),
            pl.BlockSpec((D_MODEL, tf), lambda i, f: (0, f)),
            pl.BlockSpec((tf, D_MODEL), lambda i, f: (f, 0)),
        ],
        out_specs=pl.BlockSpec((tm, D_MODEL), row),
        scratch_shapes=[pltpu.VMEM((tm, D_MODEL), BF16)],
        compiler_params=_cparams(("arbitrary", "arbitrary")),
        name="mlp",
    )(x, gpre, gpost, wup, wdn)


ODD_TN = 1024


def _odd_proj_kernel(x_ref, gpre_ref, w_ref, *rest, rope, n_q_blocks):
    if rope:
        c_ref, s_ref, o_ref, hn_sc = rest
    else:
        o_ref, hn_sc = rest
    j = pl.program_id(1)

    @pl.when(j == 0)
    def _():
        hn_sc[...] = _rms(x_ref[...], gpre_ref[...]).astype(BF16)

    acc = _dot(hn_sc[...], w_ref[...])
    if not rope:
        o_ref[...] = acc.astype(BF16)
        return
    scale = jnp.where(j < n_q_blocks, 1.0, C_QK_DIM ** -0.5).astype(F32)
    c = c_ref[...] * scale
    s = s_ref[...] * scale
    for h in range(ODD_TN // C_QK_DIM):
        x0 = acc[:, h * 256:h * 256 + 128]
        x1 = acc[:, h * 256 + 128:(h + 1) * 256]
        o_ref[:, h * 256:h * 256 + 128] = (x0 * c - x1 * s).astype(BF16)
        o_ref[:, h * 256 + 128:(h + 1) * 256] = (x0 * s + x1 * c).astype(BF16)


def _odd_proj(x, gpre, w, *, tm, seq=None, tables=None):
    T = x.shape[0]
    N = w.shape[1]
    rope = tables is not None
    in_specs = [
        pl.BlockSpec((tm, D_MODEL), lambda i, j: (i, 0)),
        pl.BlockSpec((1, D_MODEL), lambda i, j: (0, 0)),
        pl.BlockSpec((D_MODEL, ODD_TN), lambda i, j: (0, j)),
    ]
    args = [x, gpre, w]
    if rope:
        nblk_seq = seq // tm
        in_specs += [pl.BlockSpec((tm, 128), lambda i, j: (i % nblk_seq, 0))] * 2
        args += list(tables)
    return pl.pallas_call(
        functools.partial(_odd_proj_kernel, rope=rope, n_q_blocks=N // ODD_TN // 2),
        out_shape=jax.ShapeDtypeStruct((T, N), BF16),
        grid=(T // tm, N // ODD_TN),
        in_specs=in_specs,
        out_specs=pl.BlockSpec((tm, ODD_TN), lambda i, j: (i, j)),
        scratch_shapes=[pltpu.VMEM((tm, D_MODEL), BF16)],
        compiler_params=_cparams(("arbitrary", "arbitrary")),
        name="odd_qk_proj" if rope else "odd_vg_proj",
    )(*args)


def _ret_fwd_kernel(lgf_ref, lgb_ref, q_ref, k_ref, v_ref, y_ref, st_sc, d_sc, qd_sc, kd_sc, cd_sc, *, C, n_chunks):
    h = pl.program_id(1)
    lgf = lgf_ref[h]
    lgb = lgb_ref[h]

    @pl.when(pl.program_id(2) == 0)
    def _():
        st_sc[...] = jnp.zeros(st_sc.shape, F32)
        cd_sc[...] = jnp.exp(jnp.full(cd_sc.shape, C, F32) * lgf)
        i = lax.broadcasted_iota(jnp.int32, (C, C), 0)
        j = lax.broadcasted_iota(jnp.int32, (C, C), 1)
        rel = (i - j).astype(F32)
        d_sc[...] = jnp.where(rel >= 0, jnp.exp(rel * lgf), jnp.exp(-rel * lgb))
        idx = lax.broadcasted_iota(jnp.int32, (C, C_QK_DIM), 0).astype(F32)
        qd_sc[...] = jnp.exp((idx + 1.0) * lgf)
        kd_sc[...] = jnp.exp((C - 1.0 - idx) * lgf)

    st = st_sc[...]
    for c in range(n_chunks):
        rows = slice(c * C, (c + 1) * C)
        q, k, v = q_ref[rows, :], k_ref[rows, :], v_ref[rows, :]
        a = (_dot_nt(q, k) * d_sc[...]).astype(BF16)
        qd = (q.astype(F32) * qd_sc[...]).astype(BF16)
        y_ref[rows, :] = _dot(a, v) + _dot(qd, st.astype(BF16))
        kd = (k.astype(F32) * kd_sc[...]).astype(BF16)
        st = st * cd_sc[...] + _dot_tn(kd, v)
    st_sc[...] = st


def _ret_bwd_kernel(lgb_ref, q_ref, k_ref, v_ref, g_ref, yp_ref, o_ref, st_sc, qd_sc, kd_sc, cd_sc, *, C, n_chunks):
    h = pl.program_id(1)
    lgb = lgb_ref[h]

    @pl.when(pl.program_id(2) == 0)
    def _():
        st_sc[...] = jnp.zeros(st_sc.shape, F32)
        cd_sc[...] = jnp.exp(jnp.full(cd_sc.shape, C, F32) * lgb)
        idx = lax.broadcasted_iota(jnp.int32, (C, C_QK_DIM), 0).astype(F32)
        qd_sc[...] = jnp.exp((C - idx) * lgb)
        kd_sc[...] = jnp.exp(idx * lgb)

    st = st_sc[...]
    for c in reversed(range(n_chunks)):
        rows = slice(c * C, (c + 1) * C)
        q, k, v = q_ref[rows, :], k_ref[rows, :], v_ref[rows, :]
        qd = (q.astype(F32) * qd_sc[...]).astype(BF16)
        y = yp_ref[rows, :] + _dot(qd, st.astype(BF16))
        kd = (k.astype(F32) * kd_sc[...]).astype(BF16)
        st = st * cd_sc[...] + _dot_tn(kd, v)
        mu = jnp.mean(y, axis=-1, keepdims=True)
        yc = y - mu
        var = jnp.mean(yc * yc, axis=-1, keepdims=True)
        yn = yc * lax.rsqrt(var + GN_EPS)
        g = g_ref[rows, :].astype(F32)
        o_ref[rows, :] = (g * jax.nn.sigmoid(g) * yn).astype(BF16)
    st_sc[...] = st


def _retention(qk, vg, lgf, lgb, *, batch, seq, C, rows):
    T = qk.shape[0]
    nb = seq // rows
    n_chunks = rows // C
    H = C_HEADS
    fwd = lambda b, h, r: b * nb + r
    bwd = lambda b, h, r: b * nb + (nb - 1 - r)
    qk_blk = (rows, C_QK_DIM)
    v_blk = (rows, C_V_DIM)
    state_scratch = [pltpu.VMEM((C_QK_DIM, C_V_DIM), F32)]
    dec_scratch = [pltpu.VMEM((C, C_QK_DIM), F32), pltpu.VMEM((C, C_QK_DIM), F32), pltpu.VMEM((1, C_V_DIM), F32)]
    y_part = pl.pallas_call(
        functools.partial(_ret_fwd_kernel, C=C, n_chunks=n_chunks),
        out_shape=jax.ShapeDtypeStruct((T, H * C_V_DIM), F32),
        grid_spec=pltpu.PrefetchScalarGridSpec(
            num_scalar_prefetch=2,
            grid=(batch, H, nb),
            in_specs=[
                pl.BlockSpec(qk_blk, lambda b, h, r, *_: (fwd(b, h, r), h)),
                pl.BlockSpec(qk_blk, lambda b, h, r, *_: (fwd(b, h, r), H + h)),
                pl.BlockSpec(v_blk, lambda b, h, r, *_: (fwd(b, h, r), h)),
            ],
            out_specs=pl.BlockSpec(v_blk, lambda b, h, r, *_: (fwd(b, h, r), h)),
            scratch_shapes=state_scratch + [pltpu.VMEM((C, C), F32)] + dec_scratch,
        ),
        compiler_params=_cparams(("arbitrary",) * 3),
        name="ret_fwd",
    )(lgf, lgb, qk, qk, vg)
    return pl.pallas_call(
        functools.partial(_ret_bwd_kernel, C=C, n_chunks=n_chunks),
        out_shape=jax.ShapeDtypeStruct((T, H * C_V_DIM), BF16),
        grid_spec=pltpu.PrefetchScalarGridSpec(
            num_scalar_prefetch=1,
            grid=(batch, H, nb),
            in_specs=[
                pl.BlockSpec(qk_blk, lambda b, h, r, *_: (bwd(b, h, r), h)),
                pl.BlockSpec(qk_blk, lambda b, h, r, *_: (bwd(b, h, r), H + h)),
                pl.BlockSpec(v_blk, lambda b, h, r, *_: (bwd(b, h, r), h)),
                pl.BlockSpec(v_blk, lambda b, h, r, *_: (bwd(b, h, r), H + h)),
                pl.BlockSpec(v_blk, lambda b, h, r, *_: (bwd(b, h, r), h)),
            ],
            out_specs=pl.BlockSpec(v_blk, lambda b, h, r, *_: (bwd(b, h, r), h)),
            scratch_shapes=state_scratch + dec_scratch,
        ),
        compiler_params=_cparams(("arbitrary",) * 3),
        name="ret_bwd",
    )(lgb, qk, qk, vg, vg, y_part)


def _deinterleave(d):
    return np.concatenate([np.arange(0, d, 2), np.arange(1, d, 2)])


def _rope_tables(seq_max):
    t = np.arange(seq_max)
    row = (t // GRID_W).astype(np.float32)
    col = (t % GRID_W).astype(np.float32)

    def ang(d_rot):
        n = d_rot // 4
        inv = jnp.asarray(ROPE_THETA, F32) ** (-jnp.arange(n, dtype=F32) / n)
        return jnp.concatenate([jnp.asarray(row)[:, None] * inv, jnp.asarray(col)[:, None] * inv], axis=-1)

    a = ang(A_HEAD_DIM)
    ca = jnp.concatenate([jnp.cos(a), jnp.cos(a)], axis=-1)
    sa = jnp.concatenate([-jnp.sin(a), jnp.sin(a)], axis=-1)
    b = ang(B_ROPE_DIM)
    z = jnp.zeros_like(b)
    cb = jnp.concatenate([jnp.cos(b), z, jnp.cos(b), z], axis=-1)
    sb = jnp.concatenate([-jnp.sin(b), z, jnp.sin(b), z], axis=-1)
    c = ang(C_QK_DIM)
    return ca, sa, cb, sb, jnp.cos(c), jnp.sin(c)


def _prep_even(w_in, g_qa, g_ka, g_cq, w_uq, g_ckv, w_ukv, w_out):
    perm_a = _deinterleave(A_HEAD_DIM)
    perm_b = _deinterleave(B_ROPE_DIM)
    o = 0
    wqa = w_in[:, o:o + 1024].reshape(D_MODEL, A_HEADS, 128)[:, :, perm_a].reshape(D_MODEL, 1024)
    o += 1024
    wka = w_in[:, o:o + 256].reshape(D_MODEL, A_KV_HEADS, 128)[:, :, perm_a].reshape(D_MODEL, 256)
    o += 256
    wva = w_in[:, o:o + 256]
    o += 256
    wcq = w_in[:, o:o + 512]
    o += 512
    wckv = w_in[:, o:o + 256]
    o += 256
    wkr = w_in[:, o:o + 64][:, perm_b]
    z32 = jnp.zeros((D_MODEL, 32), w_in.dtype)
    wkr = jnp.concatenate([wkr[:, :32], z32, wkr[:, 32:], z32], axis=-1)
    w1 = jnp.concatenate([wqa, wka, wva, wcq, wckv, wkr], axis=-1).astype(BF16)
    uq = w_uq.reshape(B_Q_RANK, B_HEADS, B_NOPE_DIM + B_ROPE_DIM)
    uq_nope = uq[:, :, :B_NOPE_DIM].reshape(B_Q_RANK, 1024)
    uq_rope = uq[:, :, B_NOPE_DIM:][:, :, perm_b]
    zr = jnp.zeros((B_Q_RANK, B_HEADS, 32), w_uq.dtype)
    uq_rope = jnp.concatenate([uq_rope[:, :, :32], zr, uq_rope[:, :, 32:], zr], axis=-1).reshape(B_Q_RANK, 1024)
    wuq = jnp.concatenate([uq_nope, uq_rope], axis=-1).astype(BF16)
    ukv = w_ukv.reshape(B_KV_RANK, B_HEADS, B_NOPE_DIM + B_V_DIM)
    wukv = jnp.concatenate([ukv[:, :, :B_NOPE_DIM].reshape(B_KV_RANK, 1024),
                            ukv[:, :, B_NOPE_DIM:].reshape(B_KV_RANK, 1024)], axis=-1).astype(BF16)
    return dict(w1=w1, gqa=g_qa[perm_a][None], gka=g_ka[perm_a][None], gcq=g_cq[None], gckv=g_ckv[None],
                wuq=wuq, wukv=wukv, wout=w_out.astype(BF16))


def _prep_odd(w_in, decay_f, decay_b, w_out):
    perm_c = _deinterleave(C_QK_DIM)
    nqk = C_HEADS * C_QK_DIM
    wq = w_in[:, :nqk].reshape(D_MODEL, C_HEADS, C_QK_DIM)[:, :, perm_c].reshape(D_MODEL, nqk)
    wk = w_in[:, nqk:2 * nqk].reshape(D_MODEL, C_HEADS, C_QK_DIM)[:, :, perm_c].reshape(D_MODEL, nqk)
    return dict(wqk=jnp.concatenate([wq, wk], axis=-1).astype(BF16), wvg=w_in[:, 2 * nqk:].astype(BF16),
                lgf=jax.nn.log_sigmoid(decay_f.astype(F32)), lgb=jax.nn.log_sigmoid(decay_b.astype(F32)),
                wout=w_out.astype(BF16))


def _trunk(x, *, batch, seq, tables, norms, even, odd, wup, wdn):
    ca, sa, cb, sb, cc, sc = tables
    g_mix_pre, g_mix_post, g_mlp_pre, g_mlp_post = norms
    for layer in range(DEPTH):
        gpre = g_mix_pre[layer][None]
        gpost = g_mix_post[layer][None]
        if layer % 2 == 0:
            e = even[layer // 2]
            qa, ka, va, qbn, qbr, kbn, vb, kr = _even_proj(
                x, gpre, e["w1"], e["gqa"], e["gka"], e["gcq"], e["gckv"], e["wuq"], e["wukv"],
                ca, sa, cb, sb, seq=seq, tm=512)
            tk, rows_per_step = (seq, 16 * ATTN_ROWS) if seq <= 2048 else (4096, 8 * ATTN_ROWS)
            oa = _attention(qa, ka, va, batch=batch, seq=seq, G=A_HEADS, heads_per_kv=A_HEADS // A_KV_HEADS,
                            tq=rows_per_step // A_HEADS, tk=tk)
            ob = _attention(qbn, kbn, vb, batch=batch, seq=seq, G=B_HEADS // 2, heads_per_kv=1,
                            tq=rows_per_step // (B_HEADS // 2), tk=tk, qr=qbr, kr=kr)
            x = _out_proj(x, gpost, e["wout"], [(oa, 1024, 0), (ob, 1024, 0)], tm=512)
        else:
            o = odd[layer // 2]
            qk = _odd_proj(x, gpre, o["wqk"], tm=1024, seq=seq, tables=(cc, sc))
            vg = _odd_proj(x, gpre, o["wvg"], tm=1024)
            yg = _retention(qk, vg, o["lgf"], o["lgb"], batch=batch, seq=seq, C=256, rows=2048)
            x = _out_proj(x, gpost, o["wout"], [(yg, 4096, 0)], tm=512)
        x = _mlp(x, g_mlp_pre[layer][None], g_mlp_post[layer][None], wup[layer], wdn[layer], tm=1024, tf=512)
    return x


def kernel(x_prompt, x_sample, g_mix_pre, g_mix_post, g_mlp_pre, g_mlp_post, w_in_even, g_qnorm_a, g_knorm_a,
           g_cq_b, w_uq_b, g_ckv_b, w_ukv_b, w_out_even, w_in_odd, decay_fwd, decay_bwd, w_out_odd,
           w_mlp_up, w_mlp_down):
    n_even, n_odd = w_in_even.shape[0], w_in_odd.shape[0]
    even = [_prep_even(w_in_even[e], g_qnorm_a[e], g_knorm_a[e], g_cq_b[e], w_uq_b[e], g_ckv_b[e], w_ukv_b[e],
                       w_out_even[e]) for e in range(n_even)]
    odd = [_prep_odd(w_in_odd[o], decay_fwd[o], decay_bwd[o], w_out_odd[o]) for o in range(n_odd)]
    wup = w_mlp_up.astype(BF16)
    wdn = w_mlp_down.astype(BF16)
    norms = (g_mix_pre, g_mix_post, g_mlp_pre, g_mlp_post)
    tables = _rope_tables(max(x_prompt.shape[1], x_sample.shape[1]))
    outs = []
    for xg in (x_prompt, x_sample):
        b, s, d = xg.shape
        y = _trunk(xg.reshape(b * s, d), batch=b, seq=s, tables=tables, norms=norms, even=even, odd=odd,
                   wup=wup, wdn=wdn)
        outs.append(y.reshape(b, s, d))
    return tuple(outs)
```

```python
import functools

import jax
import jax.numpy as jnp
import numpy as np
from jax import lax
from jax.experimental import pallas as pl
from jax.experimental.pallas import tpu as pltpu

D_MODEL = 2048
DEPTH = 4
GRID_W = 64
ROPE_THETA = 10000.0
NORM_EPS = 1e-6
GN_EPS = 1e-5
A_HEADS, A_KV_HEADS, A_HEAD_DIM = 8, 2, 128
B_HEADS, B_Q_RANK, B_KV_RANK, B_NOPE_DIM, B_ROPE_DIM, B_V_DIM = 8, 512, 256, 128, 64, 128
C_HEADS, C_QK_DIM, C_V_DIM = 8, 256, 512
D_FF = 4 * D_MODEL

VMEM_LIMIT_BYTES = 56 * 1024 * 1024
ATTN_ROWS = 256
LOG2E = 1.4426950408889634

BF16 = jnp.bfloat16
F32 = jnp.float32


def _cparams(semantics):
    return pltpu.CompilerParams(dimension_semantics=semantics, vmem_limit_bytes=VMEM_LIMIT_BYTES)


def _rms(x, g):
    return x * lax.rsqrt(jnp.mean(x * x, axis=-1, keepdims=True) + NORM_EPS) * g


def _dot(a, b):
    return jnp.dot(a, b, preferred_element_type=F32)


def _dot_nt(a, b):
    return lax.dot_general(a, b, (((1,), (1,)), ((), ())), preferred_element_type=F32)


def _dot_tn(a, b):
    return lax.dot_general(a, b, (((0,), (0,)), ((), ())), preferred_element_type=F32)


def _rope_half(x, c, s):
    return x * c + pltpu.roll(x, 64, 1) * s


EVEN_W1_COLS = 1024 + 256 + 256 + 512 + 256 + 128


def _even_proj_kernel(x_ref, gpre_ref, w1_ref, gqa_ref, gka_ref, gcq_ref, gckv_ref, wuq_ref, wukv_ref,
                      ca_ref, sa_ref, cb_ref, sb_ref,
                      qa_ref, ka_ref, va_ref, qbn_ref, qbr_ref, kbn_ref, vb_ref, kr_ref):
    hn = _rms(x_ref[...], gpre_ref[...]).astype(BF16)
    p = _dot(hn, w1_ref[...])
    ca, sa = ca_ref[...], sa_ref[...]
    cb, sb = cb_ref[...], sb_ref[...]
    gq = gqa_ref[...] * (A_HEAD_DIM ** -0.5 * LOG2E)
    gk = gka_ref[...]
    for h in range(A_HEADS):
        xh = p[:, h * 128:(h + 1) * 128]
        qa_ref[:, h * 128:(h + 1) * 128] = _rope_half(_rms(xh, gq), ca, sa).astype(BF16)
    for h in range(A_KV_HEADS):
        xh = p[:, 1024 + h * 128:1024 + (h + 1) * 128]
        ka_ref[:, h * 128:(h + 1) * 128] = _rope_half(_rms(xh, gk), ca, sa).astype(BF16)
    va_ref[...] = p[:, 1280:1536].astype(BF16)
    cqn = _rms(p[:, 1536:2048], gcq_ref[...]).astype(BF16)
    qb = _dot(cqn, wuq_ref[...]) * ((B_NOPE_DIM + B_ROPE_DIM) ** -0.5 * LOG2E)
    qbn_ref[...] = qb[:, :1024].astype(BF16)
    for h in range(B_HEADS):
        xh = qb[:, 1024 + h * 128:1024 + (h + 1) * 128]
        qbr_ref[:, h * 128:(h + 1) * 128] = _rope_half(xh, cb, sb).astype(BF16)
    ckvn = _rms(p[:, 2048:2304], gckv_ref[...]).astype(BF16)
    kv = _dot(ckvn, wukv_ref[...])
    kbn_ref[...] = kv[:, :1024].astype(BF16)
    vb_ref[...] = kv[:, 1024:].astype(BF16)
    kr_ref[...] = _rope_half(p[:, 2304:2432], cb, sb).astype(BF16)


def _even_proj(x, gpre, w1, gqa, gka, gcq, gckv, wuq, wukv, ca, sa, cb, sb, *, seq, tm):
    T = x.shape[0]
    nblk_seq = seq // tm
    row = lambda i: (i, 0)
    fixed = lambda i: (0, 0)
    pos = lambda i: (i % nblk_seq, 0)
    out_cols = (1024, 256, 256, 1024, 1024, 1024, 1024, 128)
    return pl.pallas_call(
        _even_proj_kernel,
        out_shape=[jax.ShapeDtypeStruct((T, c), BF16) for c in out_cols],
        grid=(T // tm,),
        in_specs=[
            pl.BlockSpec((tm, D_MODEL), row),
            pl.BlockSpec((1, D_MODEL), fixed),
            pl.BlockSpec((D_MODEL, EVEN_W1_COLS), fixed, pipeline_mode=pl.Buffered(1)),
            pl.BlockSpec((1, 128), fixed),
            pl.BlockSpec((1, 128), fixed),
            pl.BlockSpec((1, B_Q_RANK), fixed),
            pl.BlockSpec((1, B_KV_RANK), fixed),
            pl.BlockSpec((B_Q_RANK, 2048), fixed, pipeline_mode=pl.Buffered(1)),
            pl.BlockSpec((B_KV_RANK, 2048), fixed, pipeline_mode=pl.Buffered(1)),
            pl.BlockSpec((tm, 128), pos),
            pl.BlockSpec((tm, 128), pos),
            pl.BlockSpec((tm, 128), pos),
            pl.BlockSpec((tm, 128), pos),
        ],
        out_specs=[pl.BlockSpec((tm, c), row) for c in out_cols],
        compiler_params=_cparams(("arbitrary",)),
        name="even_proj",
    )(x, gpre, w1, gqa, gka, gcq, gckv, wuq, wukv, ca, sa, cb, sb)


def _attn_kernel(*refs, G, heads_per_kv, rope_ext, nk):
    if rope_ext:
        q_ref, qr_ref, k_ref, kr_ref, v_ref, o_ref = refs[:6]
        scratch = refs[6:]
    else:
        q_ref, k_ref, v_ref, o_ref = refs[:4]
        qr_ref = kr_ref = None
        scratch = refs[4:]
    ki = pl.program_id(3)
    tq = q_ref.shape[0]
    items = [(g, slice(r, r + ATTN_ROWS)) for g in range(G) for r in range(0, tq, ATTN_ROWS)]

    def scores(g, rows):
        sl = slice(g * 128, (g + 1) * 128)
        kvh = g // heads_per_kv
        q, k = q_ref[rows, sl], k_ref[:, kvh * 128:(kvh + 1) * 128]
        if rope_ext:
            q = jnp.concatenate([q, qr_ref[rows, sl]], axis=-1)
            k = jnp.concatenate([k, kr_ref[...]], axis=-1)
        return _dot_nt(q, k)

    def weighted_values(g, p, v_pairs):
        kvh = g // heads_per_kv
        o = _dot(p.astype(BF16), v_pairs[kvh // 2])
        return o[:, (kvh % 2) * 128:(kvh % 2 + 1) * 128]

    n_pairs = v_ref.shape[1] // 256

    if nk == 1:
        v_pairs = [v_ref[:, i * 256:(i + 1) * 256] for i in range(n_pairs)]
        for g, rows in items:
            s = scores(g, rows)
            m = jnp.max(s, axis=-1, keepdims=True)
            p = jnp.exp2(s - m)
            l = jnp.sum(p, axis=-1, keepdims=True)
            o_ref[rows, g * 128:(g + 1) * 128] = (weighted_values(g, p, v_pairs) / l).astype(o_ref.dtype)
        return

    m_sc, l_sc, acc_sc = scratch

    @pl.when(ki == 0)
    def _():
        m_sc[...] = jnp.full(m_sc.shape, -jnp.inf, F32)
        l_sc[...] = jnp.zeros(l_sc.shape, F32)
        acc_sc[...] = jnp.zeros(acc_sc.shape, F32)

    neg_v_pairs = [-v_ref[:, i * 256:(i + 1) * 256] for i in range(n_pairs)]
    for g, rows in items:
        s = scores(g, rows)
        m_prev = m_sc[g, rows]
        m_new = jnp.maximum(m_prev, jnp.max(s, axis=-1, keepdims=True))
        alpha = jnp.exp2(m_prev - m_new)
        p = jnp.exp2(s - m_new)
        l_sc[g, rows] = alpha * l_sc[g, rows] + jnp.sum(p, axis=-1, keepdims=True)
        acc_sc[g, rows] = alpha * acc_sc[g, rows] - weighted_values(g, p, neg_v_pairs)
        m_sc[g, rows] = m_new

    @pl.when(ki == nk - 1)
    def _():
        for g in range(G):
            o_ref[:, g * 128:(g + 1) * 128] = (acc_sc[g] / l_sc[g]).astype(o_ref.dtype)


def _attention(q, k, v, *, batch, seq, G, heads_per_kv, tq, tk, qr=None, kr=None):
    T, qw = q.shape
    n_groups = qw // (G * 128)
    kvw = G // heads_per_kv * 128
    assert kvw % 256 == 0
    nq, nk = seq // tq, seq // tk
    rope_ext = qr is not None
    q_map = lambda b, h, qi, ki: (b * nq + qi, h)
    kv_map = lambda b, h, qi, ki: (b * nk + ki, h)
    kr_map = lambda b, h, qi, ki: (b * nk + ki, 0)
    in_specs = [pl.BlockSpec((tq, G * 128), q_map)]
    args = [q]
    if rope_ext:
        in_specs.append(pl.BlockSpec((tq, G * 128), q_map))
        args.append(qr)
    in_specs.append(pl.BlockSpec((tk, kvw), kv_map))
    args.append(k)
    if rope_ext:
        in_specs.append(pl.BlockSpec((tk, 128), kr_map))
        args.append(kr)
    in_specs.append(pl.BlockSpec((tk, kvw), kv_map))
    args.append(v)
    scratch = []
    if nk > 1:
        scratch = [pltpu.VMEM((G, tq, 1), F32), pltpu.VMEM((G, tq, 1), F32), pltpu.VMEM((G, tq, 128), F32)]
    return pl.pallas_call(
        functools.partial(_attn_kernel, G=G, heads_per_kv=heads_per_kv, rope_ext=rope_ext, nk=nk),
        out_shape=jax.ShapeDtypeStruct((T, qw), BF16),
        grid=(batch, n_groups, nq, nk),
        in_specs=in_specs,
        out_specs=pl.BlockSpec((tq, G * 128), q_map),
        scratch_shapes=scratch,
        compiler_params=_cparams(("arbitrary",) * 4),
        name="attn_mla" if rope_ext else "attn_gqa",
    )(*args)


def _out_proj_kernel(*refs, widths):
    x_ref, g_ref, w_ref = refs[:3]
    a_refs = refs[3:3 + len(widths)]
    o_ref = refs[3 + len(widths)]
    acc = None
    off = 0
    for a_ref, wd in zip(a_refs, widths):
        part = _dot(a_ref[...], w_ref[off:off + wd, :])
        acc = part if acc is None else acc + part
        off += wd
    o_ref[...] = x_ref[...] + _rms(acc, g_ref[...])


def _out_proj(x, g, w, parts, *, tm):
    T = x.shape[0]
    widths = tuple(wd for _, wd, _ in parts)
    row = lambda i: (i, 0)
    return pl.pallas_call(
        functools.partial(_out_proj_kernel, widths=widths),
        out_shape=jax.ShapeDtypeStruct((T, D_MODEL), F32),
        grid=(T // tm,),
        in_specs=[
            pl.BlockSpec((tm, D_MODEL), row),
            pl.BlockSpec((1, D_MODEL), lambda i: (0, 0)),
            pl.BlockSpec(w.shape, lambda i: (0, 0), pipeline_mode=pl.Buffered(1)),
        ] + [pl.BlockSpec((tm, wd), functools.partial(lambda i, cb: (i, cb), cb=cb)) for _, wd, cb in parts],
        out_specs=pl.BlockSpec((tm, D_MODEL), row),
        compiler_params=_cparams(("arbitrary",)),
        name="out_proj",
    )(x, g, w, *[a for a, _, _ in parts])


def _mlp_kernel(x_ref, gpre_ref, gpost_ref, wup_ref, wdn_ref, o_ref, hn_sc, *, nf):
    f = pl.program_id(1)

    @pl.when(f == 0)
    def _():
        hn_sc[...] = _rms(x_ref[...], gpre_ref[...]).astype(BF16)
        o_ref[...] = jnp.zeros(o_ref.shape, F32)

    h = jnp.maximum(_dot(hn_sc[...], wup_ref[...]), 0.0)
    o_ref[...] += _dot((h * h).astype(BF16), wdn_ref[...])

    @pl.when(f == nf - 1)
    def _():
        o_ref[...] = x_ref[...] + _rms(o_ref[...], gpost_ref[...])


def _mlp(x, gpre, gpost, wup, wdn, *, tm, tf):
    T = x.shape[0]
    nf = D_FF // tf
    row = lambda i, f: (i, 0)
    fixed = lambda i, f: (0, 0)
    return pl.pallas_call(
        functools.partial(_mlp_kernel, nf=nf),
        out_shape=jax.ShapeDtypeStruct((T, D_MODEL), F32),
        grid=(T // tm, nf),
        in_specs=[
            pl.BlockSpec((tm, D_MODEL), row),
            pl.BlockSpec((1, D_MODEL), fixed),
            pl.BlockSpec((1, D_MODEL), fixed),
            pl.BlockSpec((D_MODEL, tf), lambda i, f: (0, f)),
            pl.BlockSpec((tf, D_MODEL), lambda i, f: (f, 0)),
        ],
        out_specs=pl.BlockSpec((tm, D_MODEL), row),
        scratch_shapes=[pltpu.VMEM((tm, D_MODEL), BF16)],
        compiler_params=_cparams(("arbitrary", "arbitrary")),
        name="mlp",
    )(x, gpre, gpost, wup, wdn)


ODD_TN = 1024


def _odd_proj_kernel(x_ref, gpre_ref, w_ref, *rest, rope, n_q_blocks):
    if rope:
        c_ref, s_ref, o_ref, hn_sc = rest
    else:
        o_ref, hn_sc = rest
    j = pl.program_id(1)

    @pl.when(j == 0)
    def _():
        hn_sc[...] = _rms(x_ref[...], gpre_ref[...]).astype(BF16)

    acc = _dot(hn_sc[...], w_ref[...])
    if not rope:
        o_ref[...] = acc.astype(BF16)
        return
    scale = jnp.where(j < n_q_blocks, 1.0, C_QK_DIM ** -0.5).astype(F32)
    c = c_ref[...] * scale
    s = s_ref[...] * scale
    for h in range(ODD_TN // C_QK_DIM):
        x0 = acc[:, h * 256:h * 256 + 128]
        x1 = acc[:, h * 256 + 128:(h + 1) * 256]
        o_ref[:, h * 256:h * 256 + 128] = (x0 * c - x1 * s).astype(BF16)
        o_ref[:, h * 256 + 128:(h + 1) * 256] = (x0 * s + x1 * c).astype(BF16)


def _odd_proj(x, gpre, w, *, tm, seq=None, tables=None):
    T = x.shape[0]
    N = w.shape[1]
    rope = tables is not None
    in_specs = [
        pl.BlockSpec((tm, D_MODEL), lambda i, j: (i, 0)),
        pl.BlockSpec((1, D_MODEL), lambda i, j: (0, 0)),
        pl.BlockSpec((D_MODEL, ODD_TN), lambda i, j: (0, j)),
    ]
    args = [x, gpre, w]
    if rope:
        nblk_seq = seq // tm
        in_specs += [pl.BlockSpec((tm, 128), lambda i, j: (i % nblk_seq, 0))] * 2
        args += list(tables)
    return pl.pallas_call(
        functools.partial(_odd_proj_kernel, rope=rope, n_q_blocks=N // ODD_TN // 2),
        out_shape=jax.ShapeDtypeStruct((T, N), BF16),
        grid=(T // tm, N // ODD_TN),
        in_specs=in_specs,
        out_specs=pl.BlockSpec((tm, ODD_TN), lambda i, j: (i, j)),
        scratch_shapes=[pltpu.VMEM((tm, D_MODEL), BF16)],
        compiler_params=_cparams(("arbitrary", "arbitrary")),
        name="odd_qk_proj" if rope else "odd_vg_proj",
    )(*args)


def _ret_fwd_kernel(lgf_ref, lgb_ref, q_ref, k_ref, v_ref, y_ref, st_sc, d_sc, qd_sc, kd_sc, cd_sc, *, C, n_chunks):
    h = pl.program_id(1)
    lgf = lgf_ref[h]
    lgb = lgb_ref[h]

    @pl.when(pl.program_id(2) == 0)
    def _():
        st_sc[...] = jnp.zeros(st_sc.shape, F32)
        cd_sc[...] = jnp.exp(jnp.full(cd_sc.shape, C, F32) * lgf)
        i = lax.broadcasted_iota(jnp.int32, (C, C), 0)
        j = lax.broadcasted_iota(jnp.int32, (C, C), 1)
        rel = (i - j).astype(F32)
        d_sc[...] = jnp.where(rel >= 0, jnp.exp(rel * lgf), jnp.exp(-rel * lgb))
        idx = lax.broadcasted_iota(jnp.int32, (C, C_QK_DIM), 0).astype(F32)
        qd_sc[...] = jnp.exp((idx + 1.0) * lgf)
        kd_sc[...] = jnp.exp((C - 1.0 - idx) * lgf)

    st = st_sc[...]
    for c in range(n_chunks):
        rows = slice(c * C, (c + 1) * C)
        q, k, v = q_ref[rows, :], k_ref[rows, :], v_ref[rows, :]
        a = (_dot_nt(q, k) * d_sc[...]).astype(BF16)
        qd = (q.astype(F32) * qd_sc[...]).astype(BF16)
        y_ref[rows, :] = (_dot(a, v) + _dot(qd, st.astype(BF16))).astype(y_ref.dtype)
        kd = (k.astype(F32) * kd_sc[...]).astype(BF16)
        st = st * cd_sc[...] + _dot_tn(kd, v)
    st_sc[...] = st


def _ret_bwd_kernel(lgb_ref, q_ref, k_ref, v_ref, g_ref, yp_ref, o_ref, st_sc, qd_sc, kd_sc, cd_sc, *, C, n_chunks):
    h = pl.program_id(1)
    lgb = lgb_ref[h]

    @pl.when(pl.program_id(2) == 0)
    def _():
        st_sc[...] = jnp.zeros(st_sc.shape, F32)
        cd_sc[...] = jnp.exp(jnp.full(cd_sc.shape, C, F32) * lgb)
        idx = lax.broadcasted_iota(jnp.int32, (C, C_QK_DIM), 0).astype(F32)
        qd_sc[...] = jnp.exp((C - idx) * lgb)
        kd_sc[...] = jnp.exp(idx * lgb)

    st = st_sc[...]
    for c in reversed(range(n_chunks)):
        rows = slice(c * C, (c + 1) * C)
        q, k, v = q_ref[rows, :], k_ref[rows, :], v_ref[rows, :]
        qd = (q.astype(F32) * qd_sc[...]).astype(BF16)
        y = yp_ref[rows, :].astype(F32) + _dot(qd, st.astype(BF16))
        kd = (k.astype(F32) * kd_sc[...]).astype(BF16)
        st = st * cd_sc[...] + _dot_tn(kd, v)
        mu = jnp.mean(y, axis=-1, keepdims=True)
        yc = y - mu
        var = jnp.mean(yc * yc, axis=-1, keepdims=True)
        yn = yc * lax.rsqrt(var + GN_EPS)
        g = g_ref[rows, :].astype(F32)
        o_ref[rows, :] = (g * jax.nn.sigmoid(g) * yn).astype(BF16)
    st_sc[...] = st


def _retention(qk, vg, lgf, lgb, *, batch, seq, C, rows):
    T = qk.shape[0]
    nb = seq // rows
    n_chunks = rows // C
    H = C_HEADS
    fwd = lambda b, h, r: b * nb + r
    bwd = lambda b, h, r: b * nb + (nb - 1 - r)
    qk_blk = (rows, C_QK_DIM)
    v_blk = (rows, C_V_DIM)
    state_scratch = [pltpu.VMEM((C_QK_DIM, C_V_DIM), F32)]
    dec_scratch = [pltpu.VMEM((C, C_QK_DIM), F32), pltpu.VMEM((C, C_QK_DIM), F32), pltpu.VMEM((1, C_V_DIM), F32)]
    y_part = pl.pallas_call(
        functools.partial(_ret_fwd_kernel, C=C, n_chunks=n_chunks),
        out_shape=jax.ShapeDtypeStruct((T, H * C_V_DIM), BF16),
        grid_spec=pltpu.PrefetchScalarGridSpec(
            num_scalar_prefetch=2,
            grid=(batch, H, nb),
            in_specs=[
                pl.BlockSpec(qk_blk, lambda b, h, r, *_: (fwd(b, h, r), h)),
                pl.BlockSpec(qk_blk, lambda b, h, r, *_: (fwd(b, h, r), H + h)),
                pl.BlockSpec(v_blk, lambda b, h, r, *_: (fwd(b, h, r), h)),
            ],
            out_specs=pl.BlockSpec(v_blk, lambda b, h, r, *_: (fwd(b, h, r), h)),
            scratch_shapes=state_scratch + [pltpu.VMEM((C, C), F32)] + dec_scratch,
        ),
        compiler_params=_cparams(("arbitrary",) * 3),
        name="ret_fwd",
    )(lgf, lgb, qk, qk, vg)
    return pl.pallas_call(
        functools.partial(_ret_bwd_kernel, C=C, n_chunks=n_chunks),
        out_shape=jax.ShapeDtypeStruct((T, H * C_V_DIM), BF16),
        grid_spec=pltpu.PrefetchScalarGridSpec(
            num_scalar_prefetch=1,
            grid=(batch, H, nb),
            in_specs=[
                pl.BlockSpec(qk_blk, lambda b, h, r, *_: (bwd(b, h, r), h)),
                pl.BlockSpec(qk_blk, lambda b, h, r, *_: (bwd(b, h, r), H + h)),
                pl.BlockSpec(v_blk, lambda b, h, r, *_: (bwd(b, h, r), h)),
                pl.BlockSpec(v_blk, lambda b, h, r, *_: (bwd(b, h, r), H + h)),
                pl.BlockSpec(v_blk, lambda b, h, r, *_: (bwd(b, h, r), h)),
            ],
            out_specs=pl.BlockSpec(v_blk, lambda b, h, r, *_: (bwd(b, h, r), h)),
            scratch_shapes=state_scratch + dec_scratch,
        ),
        compiler_params=_cparams(("arbitrary",) * 3),
        name="ret_bwd",
    )(lgb, qk, qk, vg, vg, y_part)


def _deinterleave(d):
    return np.concatenate([np.arange(0, d, 2), np.arange(1, d, 2)])


def _rope_tables(seq_max):
    t = np.arange(seq_max)
    row = (t // GRID_W).astype(np.float32)
    col = (t % GRID_W).astype(np.float32)

    def ang(d_rot):
        n = d_rot // 4
        inv = jnp.asarray(ROPE_THETA, F32) ** (-jnp.arange(n, dtype=F32) / n)
        return jnp.concatenate([jnp.asarray(row)[:, None] * inv, jnp.asarray(col)[:, None] * inv], axis=-1)

    a = ang(A_HEAD_DIM)
    ca = jnp.concatenate([jnp.cos(a), jnp.cos(a)], axis=-1)
    sa = jnp.concatenate([-jnp.sin(a), jnp.sin(a)], axis=-1)
    b = ang(B_ROPE_DIM)
    z = jnp.zeros_like(b)
    cb = jnp.concatenate([jnp.cos(b), z, jnp.cos(b), z], axis=-1)
    sb = jnp.concatenate([-jnp.sin(b), z, jnp.sin(b), z], axis=-1)
    c = ang(C_QK_DIM)
    return ca, sa, cb, sb, jnp.cos(c), jnp.sin(c)


def _prep_even(w_in, g_qa, g_ka, g_cq, w_uq, g_ckv, w_ukv, w_out):
    perm_a = _deinterleave(A_HEAD_DIM)
    perm_b = _deinterleave(B_ROPE_DIM)
    o = 0
    wqa = w_in[:, o:o + 1024].reshape(D_MODEL, A_HEADS, 128)[:, :, perm_a].reshape(D_MODEL, 1024)
    o += 1024
    wka = w_in[:, o:o + 256].reshape(D_MODEL, A_KV_HEADS, 128)[:, :, perm_a].reshape(D_MODEL, 256)
    o += 256
    wva = w_in[:, o:o + 256]
    o += 256
    wcq = w_in[:, o:o + 512]
    o += 512
    wckv = w_in[:, o:o + 256]
    o += 256
    wkr = w_in[:, o:o + 64][:, perm_b]
    z32 = jnp.zeros((D_MODEL, 32), w_in.dtype)
    wkr = jnp.concatenate([wkr[:, :32], z32, wkr[:, 32:], z32], axis=-1)
    w1 = jnp.concatenate([wqa, wka, wva, wcq, wckv, wkr], axis=-1).astype(BF16)
    uq = w_uq.reshape(B_Q_RANK, B_HEADS, B_NOPE_DIM + B_ROPE_DIM)
    uq_nope = uq[:, :, :B_NOPE_DIM].reshape(B_Q_RANK, 1024)
    uq_rope = uq[:, :, B_NOPE_DIM:][:, :, perm_b]
    zr = jnp.zeros((B_Q_RANK, B_HEADS, 32), w_uq.dtype)
    uq_rope = jnp.concatenate([uq_rope[:, :, :32], zr, uq_rope[:, :, 32:], zr], axis=-1).reshape(B_Q_RANK, 1024)
    wuq = jnp.concatenate([uq_nope, uq_rope], axis=-1).astype(BF16)
    ukv = w_ukv.reshape(B_KV_RANK, B_HEADS, B_NOPE_DIM + B_V_DIM)
    wukv = jnp.concatenate([ukv[:, :, :B_NOPE_DIM].reshape(B_KV_RANK, 1024),
                            ukv[:, :, B_NOPE_DIM:].reshape(B_KV_RANK, 1024)], axis=-1).astype(BF16)
    return dict(w1=w1, gqa=g_qa[perm_a][None], gka=g_ka[perm_a][None], gcq=g_cq[None], gckv=g_ckv[None],
                wuq=wuq, wukv=wukv, wout=w_out.astype(BF16))


def _prep_odd(w_in, decay_f, decay_b, w_out):
    perm_c = _deinterleave(C_QK_DIM)
    nqk = C_HEADS * C_QK_DIM
    wq = w_in[:, :nqk].reshape(D_MODEL, C_HEADS, C_QK_DIM)[:, :, perm_c].reshape(D_MODEL, nqk)
    wk = w_in[:, nqk:2 * nqk].reshape(D_MODEL, C_HEADS, C_QK_DIM)[:, :, perm_c].reshape(D_MODEL, nqk)
    return dict(wqk=jnp.concatenate([wq, wk], axis=-1).astype(BF16), wvg=w_in[:, 2 * nqk:].astype(BF16),
                lgf=jax.nn.log_sigmoid(decay_f.astype(F32)), lgb=jax.nn.log_sigmoid(decay_b.astype(F32)),
                wout=w_out.astype(BF16))


def _trunk(x, *, batch, seq, tables, norms, even, odd, wup, wdn):
    ca, sa, cb, sb, cc, sc = tables
    g_mix_pre, g_mix_post, g_mlp_pre, g_mlp_post = norms
    for layer in range(DEPTH):
        gpre = g_mix_pre[layer][None]
        gpost = g_mix_post[layer][None]
        if layer % 2 == 0:
            e = even[layer // 2]
            qa, ka, va, qbn, qbr, kbn, vb, kr = _even_proj(
                x, gpre, e["w1"], e["gqa"], e["gka"], e["gcq"], e["gckv"], e["wuq"], e["wukv"],
                ca, sa, cb, sb, seq=seq, tm=512)
            tk, rows_per_step = (seq, 16 * ATTN_ROWS) if seq <= 2048 else (4096, 8 * ATTN_ROWS)
            oa = _attention(qa, ka, va, batch=batch, seq=seq, G=A_HEADS, heads_per_kv=A_HEADS // A_KV_HEADS,
                            tq=rows_per_step // A_HEADS, tk=tk)
            ob = _attention(qbn, kbn, vb, batch=batch, seq=seq, G=B_HEADS // 2, heads_per_kv=1,
                            tq=rows_per_step // (B_HEADS // 2), tk=tk, qr=qbr, kr=kr)
            x = _out_proj(x, gpost, e["wout"], [(oa, 1024, 0), (ob, 1024, 0)], tm=512)
        else:
            o = odd[layer // 2]
            qk = _odd_proj(x, gpre, o["wqk"], tm=1024, seq=seq, tables=(cc, sc))
            vg = _odd_proj(x, gpre, o["wvg"], tm=1024)
            yg = _retention(qk, vg, o["lgf"], o["lgb"], batch=batch, seq=seq, C=256, rows=2048)
            x = _out_proj(x, gpost, o["wout"], [(yg, 4096, 0)], tm=512)
        x = _mlp(x, g_mlp_pre[layer][None], g_mlp_post[layer][None], wup[layer], wdn[layer], tm=1024, tf=512)
    return x


def kernel(x_prompt, x_sample, g_mix_pre, g_mix_post, g_mlp_pre, g_mlp_post, w_in_even, g_qnorm_a, g_knorm_a,
           g_cq_b, w_uq_b, g_ckv_b, w_ukv_b, w_out_even, w_in_odd, decay_fwd, decay_bwd, w_out_odd,
           w_mlp_up, w_mlp_down):
    n_even, n_odd = w_in_even.shape[0], w_in_odd.shape[0]
    even = [_prep_even(w_in_even[e], g_qnorm_a[e], g_knorm_a[e], g_cq_b[e], w_uq_b[e], g_ckv_b[e], w_ukv_b[e],
                       w_out_even[e]) for e in range(n_even)]
    odd = [_prep_odd(w_in_odd[o], decay_fwd[o], decay_bwd[o], w_out_odd[o]) for o in range(n_odd)]
    wup = [w_mlp_up[layer].astype(BF16) for layer in range(DEPTH)]
    wdn = [w_mlp_down[layer].astype(BF16) for layer in range(DEPTH)]
    norms = (g_mix_pre, g_mix_post, g_mlp_pre, g_mlp_post)
    tables = _rope_tables(max(x_prompt.shape[1], x_sample.shape[1]))
    outs = []
    for xg in (x_prompt, x_sample):
        b, s, d = xg.shape
        y = _trunk(xg.reshape(b * s, d), batch=b, seq=s, tables=tables, norms=norms, even=even, odd=odd,
                   wup=wup, wdn=wdn)
        outs.append(y.reshape(b, s, d))
    return tuple(outs)
```

```python
import functools

import jax
import jax.numpy as jnp
import numpy as np
from jax import lax
from jax.experimental import pallas as pl
from jax.experimental.pallas import tpu as pltpu

D_MODEL = 2048
DEPTH = 4
GRID_W = 64
ROPE_THETA = 10000.0
NORM_EPS = 1e-6
GN_EPS = 1e-5
A_HEADS, A_KV_HEADS, A_HEAD_DIM = 8, 2, 128
B_HEADS, B_Q_RANK, B_KV_RANK, B_NOPE_DIM, B_ROPE_DIM, B_V_DIM = 8, 512, 256, 128, 64, 128
C_HEADS, C_QK_DIM, C_V_DIM = 8, 256, 512
D_FF = 4 * D_MODEL

VMEM_LIMIT_BYTES = 56 * 1024 * 1024
ATTN_ROWS = 256
LOG2E = 1.4426950408889634

BF16 = jnp.bfloat16
F32 = jnp.float32


def _cparams(semantics):
    return pltpu.CompilerParams(dimension_semantics=semantics, vmem_limit_bytes=VMEM_LIMIT_BYTES)


def _rms(x, g):
    return x * lax.rsqrt(jnp.mean(x * x, axis=-1, keepdims=True) + NORM_EPS) * g


def _dot(a, b):
    return jnp.dot(a, b, preferred_element_type=F32)


def _dot_nt(a, b):
    return lax.dot_general(a, b, (((1,), (1,)), ((), ())), preferred_element_type=F32)


def _dot_tn(a, b):
    return lax.dot_general(a, b, (((0,), (0,)), ((), ())), preferred_element_type=F32)


def _rope_half(x, c, s):
    return x * c + pltpu.roll(x, 64, 1) * s


EVEN_W1_COLS = 1024 + 256 + 256 + 512 + 256 + 128


def _even_proj_kernel(x_ref, gpre_ref, w1_ref, gqa_ref, gka_ref, gcq_ref, gckv_ref, wuq_ref, wukv_ref,
                      ca_ref, sa_ref, cb_ref, sb_ref,
                      qa_ref, ka_ref, va_ref, qbn_ref, qbr_ref, kbn_ref, vb_ref, kr_ref):
    hn = _rms(x_ref[...], gpre_ref[...]).astype(BF16)
    p = _dot(hn, w1_ref[...])
    ca, sa = ca_ref[...], sa_ref[...]
    cb, sb = cb_ref[...], sb_ref[...]
    gq = gqa_ref[...] * (A_HEAD_DIM ** -0.5 * LOG2E)
    gk = gka_ref[...]
    for h in range(A_HEADS):
        xh = p[:, h * 128:(h + 1) * 128]
        qa_ref[:, h * 128:(h + 1) * 128] = _rope_half(_rms(xh, gq), ca, sa).astype(BF16)
    for h in range(A_KV_HEADS):
        xh = p[:, 1024 + h * 128:1024 + (h + 1) * 128]
        ka_ref[:, h * 128:(h + 1) * 128] = _rope_half(_rms(xh, gk), ca, sa).astype(BF16)
    va_ref[...] = p[:, 1280:1536].astype(BF16)
    cqn = _rms(p[:, 1536:2048], gcq_ref[...]).astype(BF16)
    qb = _dot(cqn, wuq_ref[...]) * ((B_NOPE_DIM + B_ROPE_DIM) ** -0.5 * LOG2E)
    qbn_ref[...] = qb[:, :1024].astype(BF16)
    for h in range(B_HEADS):
        xh = qb[:, 1024 + h * 128:1024 + (h + 1) * 128]
        qbr_ref[:, h * 128:(h + 1) * 128] = _rope_half(xh, cb, sb).astype(BF16)
    ckvn = _rms(p[:, 2048:2304], gckv_ref[...]).astype(BF16)
    kv = _dot(ckvn, wukv_ref[...])
    kbn_ref[...] = kv[:, :1024].astype(BF16)
    vb_ref[...] = kv[:, 1024:].astype(BF16)
    kr_ref[...] = _rope_half(p[:, 2304:2432], cb, sb).astype(BF16)


def _even_proj(x, gpre, w1, gqa, gka, gcq, gckv, wuq, wukv, ca, sa, cb, sb, *, seq, tm):
    T = x.shape[0]
    nblk_seq = seq // tm
    row = lambda i: (i, 0)
    fixed = lambda i: (0, 0)
    pos = lambda i: (i % nblk_seq, 0)
    out_cols = (1024, 256, 256, 1024, 1024, 1024, 1024, 128)
    return pl.pallas_call(
        _even_proj_kernel,
        out_shape=[jax.ShapeDtypeStruct((T, c), BF16) for c in out_cols],
        grid=(T // tm,),
        in_specs=[
            pl.BlockSpec((tm, D_MODEL), row),
            pl.BlockSpec((1, D_MODEL), fixed),
            pl.BlockSpec((D_MODEL, EVEN_W1_COLS), fixed, pipeline_mode=pl.Buffered(1)),
            pl.BlockSpec((1, 128), fixed),
            pl.BlockSpec((1, 128), fixed),
            pl.BlockSpec((1, B_Q_RANK), fixed),
            pl.BlockSpec((1, B_KV_RANK), fixed),
            pl.BlockSpec((B_Q_RANK, 2048), fixed, pipeline_mode=pl.Buffered(1)),
            pl.BlockSpec((B_KV_RANK, 2048), fixed, pipeline_mode=pl.Buffered(1)),
            pl.BlockSpec((tm, 128), pos),
            pl.BlockSpec((tm, 128), pos),
            pl.BlockSpec((tm, 128), pos),
            pl.BlockSpec((tm, 128), pos),
        ],
        out_specs=[pl.BlockSpec((tm, c), row) for c in out_cols],
        compiler_params=_cparams(("arbitrary",)),
        name="even_proj",
    )(x, gpre, w1, gqa, gka, gcq, gckv, wuq, wukv, ca, sa, cb, sb)


def _attn_kernel(*refs, G, heads_per_kv, rope_ext, nk):
    if rope_ext:
        q_ref, qr_ref, k_ref, kr_ref, v_ref, o_ref = refs[:6]
        scratch = refs[6:]
    else:
        q_ref, k_ref, v_ref, o_ref = refs[:4]
        qr_ref = kr_ref = None
        scratch = refs[4:]
    ki = pl.program_id(3)
    tq = q_ref.shape[0]
    items = [(g, slice(r, r + ATTN_ROWS)) for g in range(G) for r in range(0, tq, ATTN_ROWS)]

    def scores(g, rows):
        sl = slice(g * 128, (g + 1) * 128)
        kvh = g // heads_per_kv
        q, k = q_ref[rows, sl], k_ref[:, kvh * 128:(kvh + 1) * 128]
        if rope_ext:
            q = jnp.concatenate([q, qr_ref[rows, sl]], axis=-1)
            k = jnp.concatenate([k, kr_ref[...]], axis=-1)
        return _dot_nt(q, k)

    def weighted_values(g, p, v_pairs):
        kvh = g // heads_per_kv
        o = _dot(p.astype(BF16), v_pairs[kvh // 2])
        return o[:, (kvh % 2) * 128:(kvh % 2 + 1) * 128]

    n_pairs = v_ref.shape[1] // 256

    if nk == 1:
        v_pairs = [v_ref[:, i * 256:(i + 1) * 256] for i in range(n_pairs)]
        for g, rows in items:
            s = scores(g, rows)
            m = jnp.max(s, axis=-1, keepdims=True)
            p = jnp.exp2(s - m)
            l = jnp.sum(p, axis=-1, keepdims=True)
            o_ref[rows, g * 128:(g + 1) * 128] = (weighted_values(g, p, v_pairs) / l).astype(o_ref.dtype)
        return

    m_sc, l_sc, acc_sc = scratch

    @pl.when(ki == 0)
    def _():
        m_sc[...] = jnp.full(m_sc.shape, -jnp.inf, F32)
        l_sc[...] = jnp.zeros(l_sc.shape, F32)
        acc_sc[...] = jnp.zeros(acc_sc.shape, F32)

    neg_v_pairs = [-v_ref[:, i * 256:(i + 1) * 256] for i in range(n_pairs)]
    for g, rows in items:
        s = scores(g, rows)
        m_prev = m_sc[g, rows]
        m_new = jnp.maximum(m_prev, jnp.max(s, axis=-1, keepdims=True))
        alpha = jnp.exp2(m_prev - m_new)
        p = jnp.exp2(s - m_new)
        l_sc[g, rows] = alpha * l_sc[g, rows] + jnp.sum(p, axis=-1, keepdims=True)
        acc_sc[g, rows] = alpha * acc_sc[g, rows] - weighted_values(g, p, neg_v_pairs)
        m_sc[g, rows] = m_new

    @pl.when(ki == nk - 1)
    def _():
        for g in range(G):
            o_ref[:, g * 128:(g + 1) * 128] = (acc_sc[g] / l_sc[g]).astype(o_ref.dtype)


def _attention(q, k, v, *, batch, seq, G, heads_per_kv, tq, tk, qr=None, kr=None):
    T, qw = q.shape
    n_groups = qw // (G * 128)
    kvw = G // heads_per_kv * 128
    assert kvw % 256 == 0
    nq, nk = seq // tq, seq // tk
    rope_ext = qr is not None
    q_map = lambda b, h, qi, ki: (b * nq + qi, h)
    kv_map = lambda b, h, qi, ki: (b * nk + ki, h)
    kr_map = lambda b, h, qi, ki: (b * nk + ki, 0)
    in_specs = [pl.BlockSpec((tq, G * 128), q_map)]
    args = [q]
    if rope_ext:
        in_specs.append(pl.BlockSpec((tq, G * 128), q_map))
        args.append(qr)
    in_specs.append(pl.BlockSpec((tk, kvw), kv_map))
    args.append(k)
    if rope_ext:
        in_specs.append(pl.BlockSpec((tk, 128), kr_map))
        args.append(kr)
    in_specs.append(pl.BlockSpec((tk, kvw), kv_map))
    args.append(v)
    scratch = []
    if nk > 1:
        scratch = [pltpu.VMEM((G, tq, 1), F32), pltpu.VMEM((G, tq, 1), F32), pltpu.VMEM((G, tq, 128), F32)]
    return pl.pallas_call(
        functools.partial(_attn_kernel, G=G, heads_per_kv=heads_per_kv, rope_ext=rope_ext, nk=nk),
        out_shape=jax.ShapeDtypeStruct((T, qw), BF16),
        grid=(batch, n_groups, nq, nk),
        in_specs=in_specs,
        out_specs=pl.BlockSpec((tq, G * 128), q_map),
        scratch_shapes=scratch,
        compiler_params=_cparams(("arbitrary",) * 4),
        name="attn_mla" if rope_ext else "attn_gqa",
    )(*args)


def _out_proj_kernel(*refs, widths):
    x_ref, g_ref, w_ref = refs[:3]
    a_refs = refs[3:3 + len(widths)]
    o_ref = refs[3 + len(widths)]
    acc = None
    off = 0
    for a_ref, wd in zip(a_refs, widths):
        part = _dot(a_ref[...], w_ref[off:off + wd, :])
        acc = part if acc is None else acc + part
        off += wd
    o_ref[...] = x_ref[...] + _rms(acc, g_ref[...])


def _out_proj(x, g, w, parts, *, tm):
    T = x.shape[0]
    widths = tuple(wd for _, wd, _ in parts)
    row = lambda i: (i, 0)
    return pl.pallas_call(
        functools.partial(_out_proj_kernel, widths=widths),
        out_shape=jax.ShapeDtypeStruct((T, D_MODEL), F32),
        grid=(T // tm,),
        in_specs=[
            pl.BlockSpec((tm, D_MODEL), row),
            pl.BlockSpec((1, D_MODEL), lambda i: (0, 0)),
            pl.BlockSpec(w.shape, lambda i: (0, 0), pipeline_mode=pl.Buffered(1)),
        ] + [pl.BlockSpec((tm, wd), functools.partial(lambda i, cb: (i, cb), cb=cb)) for _, wd, cb in parts],
        out_specs=pl.BlockSpec((tm, D_MODEL), row),
        compiler_params=_cparams(("arbitrary",)),
        name="out_proj",
    )(x, g, w, *[a for a, _, _ in parts])


def _mlp_kernel(x_ref, gpre_ref, gpost_ref, wup_ref, wdn_ref, o_ref, hn_sc, *, nf):
    f = pl.program_id(1)

    @pl.when(f == 0)
    def _():
        hn_sc[...] = _rms(x_ref[...], gpre_ref[...]).astype(BF16)
        o_ref[...] = jnp.zeros(o_ref.shape, F32)

    h = jnp.maximum(_dot(hn_sc[...], wup_ref[...]), 0.0)
    o_ref[...] += _dot((h * h).astype(BF16), wdn_ref[...])

    @pl.when(f == nf - 1)
    def _():
        o_ref[...] = x_ref[...] + _rms(o_ref[...], gpost_ref[...])


def _mlp(x, gpre, gpost, wup, wdn, layer, *, tm, tf):
    T = x.shape[0]
    nf = D_FF // tf
    row = lambda i, f: (i, 0)
    fixed = lambda i, f: (0, 0)
    return pl.pallas_call(
        functools.partial(_mlp_kernel, nf=nf),
        out_shape=jax.ShapeDtypeStruct((T, D_MODEL), F32),
        grid=(T // tm, nf),
        in_specs=[
            pl.BlockSpec((tm, D_MODEL), row),
            pl.BlockSpec((1, D_MODEL), fixed),
            pl.BlockSpec((1, D_MODEL), fixed),
            pl.BlockSpec((None, D_MODEL, tf), lambda i, f: (layer, 0, f)),
            pl.BlockSpec((None, tf, D_MODEL), lambda i, f: (layer, f, 0)),
        ],
        out_specs=pl.BlockSpec((tm, D_MODEL), row),
        scratch_shapes=[pltpu.VMEM((tm, D_MODEL), BF16)],
        compiler_params=_cparams(("arbitrary", "arbitrary")),
        name="mlp",
    )(x, gpre, gpost, wup, wdn)


ODD_TN = 1024


def _odd_proj_kernel(x_ref, gpre_ref, w_ref, *rest, rope, n_q_blocks):
    if rope:
        c_ref, s_ref, o_ref, hn_sc = rest
    else:
        o_ref, hn_sc = rest
    j = pl.program_id(1)

    @pl.when(j == 0)
    def _():
        hn_sc[...] = _rms(x_ref[...], gpre_ref[...]).astype(BF16)

    acc = _dot(hn_sc[...], w_ref[...])
    if not rope:
        o_ref[...] = acc.astype(BF16)
        return
    scale = jnp.where(j < n_q_blocks, 1.0, C_QK_DIM ** -0.5).astype(F32)
    c = c_ref[...] * scale
    s = s_ref[...] * scale
    for h in range(ODD_TN // C_QK_DIM):
        x0 = acc[:, h * 256:h * 256 + 128]
        x1 = acc[:, h * 256 + 128:(h + 1) * 256]
        o_ref[:, h * 256:h * 256 + 128] = (x0 * c - x1 * s).astype(BF16)
        o_ref[:, h * 256 + 128:(h + 1) * 256] = (x0 * s + x1 * c).astype(BF16)


def _odd_proj(x, gpre, w, *, tm, seq=None, tables=None):
    T = x.shape[0]
    N = w.shape[1]
    rope = tables is not None
    in_specs = [
        pl.BlockSpec((tm, D_MODEL), lambda i, j: (i, 0)),
        pl.BlockSpec((1, D_MODEL), lambda i, j: (0, 0)),
        pl.BlockSpec((D_MODEL, ODD_TN), lambda i, j: (0, j)),
    ]
    args = [x, gpre, w]
    if rope:
        nblk_seq = seq // tm
        in_specs += [pl.BlockSpec((tm, 128), lambda i, j: (i % nblk_seq, 0))] * 2
        args += list(tables)
    return pl.pallas_call(
        functools.partial(_odd_proj_kernel, rope=rope, n_q_blocks=N // ODD_TN // 2),
        out_shape=jax.ShapeDtypeStruct((T, N), BF16),
        grid=(T // tm, N // ODD_TN),
        in_specs=in_specs,
        out_specs=pl.BlockSpec((tm, ODD_TN), lambda i, j: (i, j)),
        scratch_shapes=[pltpu.VMEM((tm, D_MODEL), BF16)],
        compiler_params=_cparams(("arbitrary", "arbitrary")),
        name="odd_qk_proj" if rope else "odd_vg_proj",
    )(*args)


def _ret_fwd_kernel(lgf_ref, lgb_ref, q_ref, k_ref, v_ref, y_ref, st_sc, d_sc, qd_sc, kd_sc, cd_sc, *, C, n_chunks):
    h = pl.program_id(1)
    lgf = lgf_ref[h]
    lgb = lgb_ref[h]

    @pl.when(pl.program_id(2) == 0)
    def _():
        st_sc[...] = jnp.zeros(st_sc.shape, F32)
        cd_sc[...] = jnp.exp(jnp.full(cd_sc.shape, C, F32) * lgf)
        i = lax.broadcasted_iota(jnp.int32, (C, C), 0)
        j = lax.broadcasted_iota(jnp.int32, (C, C), 1)
        rel = (i - j).astype(F32)
        d_sc[...] = jnp.where(rel >= 0, jnp.exp(rel * lgf), jnp.exp(-rel * lgb))
        idx = lax.broadcasted_iota(jnp.int32, (C, C_QK_DIM), 0).astype(F32)
        qd_sc[...] = jnp.exp((idx + 1.0) * lgf)
        kd_sc[...] = jnp.exp((C - 1.0 - idx) * lgf)

    st = st_sc[...]
    for c in range(n_chunks):
        rows = slice(c * C, (c + 1) * C)
        q, k, v = q_ref[rows, :], k_ref[rows, :], v_ref[rows, :]
        a = (_dot_nt(q, k) * d_sc[...]).astype(BF16)
        qd = (q.astype(F32) * qd_sc[...]).astype(BF16)
        y_ref[rows, :] = (_dot(a, v) + _dot(qd, st.astype(BF16))).astype(y_ref.dtype)
        kd = (k.astype(F32) * kd_sc[...]).astype(BF16)
        st = st * cd_sc[...] + _dot_tn(kd, v)
    st_sc[...] = st


def _ret_bwd_kernel(lgb_ref, q_ref, k_ref, v_ref, g_ref, yp_ref, o_ref, st_sc, qd_sc, kd_sc, cd_sc, *, C, n_chunks):
    h = pl.program_id(1)
    lgb = lgb_ref[h]

    @pl.when(pl.program_id(2) == 0)
    def _():
        st_sc[...] = jnp.zeros(st_sc.shape, F32)
        cd_sc[...] = jnp.exp(jnp.full(cd_sc.shape, C, F32) * lgb)
        idx = lax.broadcasted_iota(jnp.int32, (C, C_QK_DIM), 0).astype(F32)
        qd_sc[...] = jnp.exp((C - idx) * lgb)
        kd_sc[...] = jnp.exp(idx * lgb)

    st = st_sc[...]
    for c in reversed(range(n_chunks)):
        rows = slice(c * C, (c + 1) * C)
        q, k, v = q_ref[rows, :], k_ref[rows, :], v_ref[rows, :]
        qd = (q.astype(F32) * qd_sc[...]).astype(BF16)
        y = yp_ref[rows, :].astype(F32) + _dot(qd, st.astype(BF16))
        kd = (k.astype(F32) * kd_sc[...]).astype(BF16)
        st = st * cd_sc[...] + _dot_tn(kd, v)
        mu = jnp.mean(y, axis=-1, keepdims=True)
        yc = y - mu
        var = jnp.mean(yc * yc, axis=-1, keepdims=True)
        yn = yc * lax.rsqrt(var + GN_EPS)
        g = g_ref[rows, :].astype(F32)
        o_ref[rows, :] = (g * jax.nn.sigmoid(g) * yn).astype(BF16)
    st_sc[...] = st


def _retention(qk, vg, lgf, lgb, *, batch, seq, C, rows):
    T = qk.shape[0]
    nb = seq // rows
    n_chunks = rows // C
    H = C_HEADS
    fwd = lambda b, h, r: b * nb + r
    bwd = lambda b, h, r: b * nb + (nb - 1 - r)
    qk_blk = (rows, C_QK_DIM)
    v_blk = (rows, C_V_DIM)
    state_scratch = [pltpu.VMEM((C_QK_DIM, C_V_DIM), F32)]
    dec_scratch = [pltpu.VMEM((C, C_QK_DIM), F32), pltpu.VMEM((C, C_QK_DIM), F32), pltpu.VMEM((1, C_V_DIM), F32)]
    y_part = pl.pallas_call(
        functools.partial(_ret_fwd_kernel, C=C, n_chunks=n_chunks),
        out_shape=jax.ShapeDtypeStruct((T, H * C_V_DIM), BF16),
        grid_spec=pltpu.PrefetchScalarGridSpec(
            num_scalar_prefetch=2,
            grid=(batch, H, nb),
            in_specs=[
                pl.BlockSpec(qk_blk, lambda b, h, r, *_: (fwd(b, h, r), h)),
                pl.BlockSpec(qk_blk, lambda b, h, r, *_: (fwd(b, h, r), H + h)),
                pl.BlockSpec(v_blk, lambda b, h, r, *_: (fwd(b, h, r), h)),
            ],
            out_specs=pl.BlockSpec(v_blk, lambda b, h, r, *_: (fwd(b, h, r), h)),
            scratch_shapes=state_scratch + [pltpu.VMEM((C, C), F32)] + dec_scratch,
        ),
        compiler_params=_cparams(("arbitrary",) * 3),
        name="ret_fwd",
    )(lgf, lgb, qk, qk, vg)
    return pl.pallas_call(
        functools.partial(_ret_bwd_kernel, C=C, n_chunks=n_chunks),
        out_shape=jax.ShapeDtypeStruct((T, H * C_V_DIM), BF16),
        grid_spec=pltpu.PrefetchScalarGridSpec(
            num_scalar_prefetch=1,
            grid=(batch, H, nb),
            in_specs=[
                pl.BlockSpec(qk_blk, lambda b, h, r, *_: (bwd(b, h, r), h)),
                pl.BlockSpec(qk_blk, lambda b, h, r, *_: (bwd(b, h, r), H + h)),
                pl.BlockSpec(v_blk, lambda b, h, r, *_: (bwd(b, h, r), h)),
                pl.BlockSpec(v_blk, lambda b, h, r, *_: (bwd(b, h, r), H + h)),
                pl.BlockSpec(v_blk, lambda b, h, r, *_: (bwd(b, h, r), h)),
            ],
            out_specs=pl.BlockSpec(v_blk, lambda b, h, r, *_: (bwd(b, h, r), h)),
            scratch_shapes=state_scratch + dec_scratch,
        ),
        compiler_params=_cparams(("arbitrary",) * 3),
        name="ret_bwd",
    )(lgb, qk, qk, vg, vg, y_part)


def _deinterleave(d):
    return np.concatenate([np.arange(0, d, 2), np.arange(1, d, 2)])


def _rope_tables(seq_max):
    t = np.arange(seq_max)
    row = (t // GRID_W).astype(np.float32)
    col = (t % GRID_W).astype(np.float32)

    def ang(d_rot):
        n = d_rot // 4
        inv = jnp.asarray(ROPE_THETA, F32) ** (-jnp.arange(n, dtype=F32) / n)
        return jnp.concatenate([jnp.asarray(row)[:, None] * inv, jnp.asarray(col)[:, None] * inv], axis=-1)

    a = ang(A_HEAD_DIM)
    ca = jnp.concatenate([jnp.cos(a), jnp.cos(a)], axis=-1)
    sa = jnp.concatenate([-jnp.sin(a), jnp.sin(a)], axis=-1)
    b = ang(B_ROPE_DIM)
    z = jnp.zeros_like(b)
    cb = jnp.concatenate([jnp.cos(b), z, jnp.cos(b), z], axis=-1)
    sb = jnp.concatenate([-jnp.sin(b), z, jnp.sin(b), z], axis=-1)
    c = ang(C_QK_DIM)
    return ca, sa, cb, sb, jnp.cos(c), jnp.sin(c)


def _prep_even(w_in, g_qa, g_ka, g_cq, w_uq, g_ckv, w_ukv, w_out):
    perm_a = _deinterleave(A_HEAD_DIM)
    perm_b = _deinterleave(B_ROPE_DIM)
    w_in, w_uq, w_ukv = w_in.astype(BF16), w_uq.astype(BF16), w_ukv.astype(BF16)
    o = 0
    wqa = w_in[:, o:o + 1024].reshape(D_MODEL, A_HEADS, 128)[:, :, perm_a].reshape(D_MODEL, 1024)
    o += 1024
    wka = w_in[:, o:o + 256].reshape(D_MODEL, A_KV_HEADS, 128)[:, :, perm_a].reshape(D_MODEL, 256)
    o += 256
    wva = w_in[:, o:o + 256]
    o += 256
    wcq = w_in[:, o:o + 512]
    o += 512
    wckv = w_in[:, o:o + 256]
    o += 256
    wkr = w_in[:, o:o + 64][:, perm_b]
    z32 = jnp.zeros((D_MODEL, 32), w_in.dtype)
    wkr = jnp.concatenate([wkr[:, :32], z32, wkr[:, 32:], z32], axis=-1)
    w1 = jnp.concatenate([wqa, wka, wva, wcq, wckv, wkr], axis=-1).astype(BF16)
    uq = w_uq.reshape(B_Q_RANK, B_HEADS, B_NOPE_DIM + B_ROPE_DIM)
    uq_nope = uq[:, :, :B_NOPE_DIM].reshape(B_Q_RANK, 1024)
    uq_rope = uq[:, :, B_NOPE_DIM:][:, :, perm_b]
    zr = jnp.zeros((B_Q_RANK, B_HEADS, 32), w_uq.dtype)
    uq_rope = jnp.concatenate([uq_rope[:, :, :32], zr, uq_rope[:, :, 32:], zr], axis=-1).reshape(B_Q_RANK, 1024)
    wuq = jnp.concatenate([uq_nope, uq_rope], axis=-1).astype(BF16)
    ukv = w_ukv.reshape(B_KV_RANK, B_HEADS, B_NOPE_DIM + B_V_DIM)
    wukv = jnp.concatenate([ukv[:, :, :B_NOPE_DIM].reshape(B_KV_RANK, 1024),
                            ukv[:, :, B_NOPE_DIM:].reshape(B_KV_RANK, 1024)], axis=-1).astype(BF16)
    return dict(w1=w1, gqa=g_qa[perm_a][None], gka=g_ka[perm_a][None], gcq=g_cq[None], gckv=g_ckv[None],
                wuq=wuq, wukv=wukv, wout=w_out.astype(BF16))


def _prep_odd(w_in, decay_f, decay_b, w_out):
    perm_c = _deinterleave(C_QK_DIM)
    nqk = C_HEADS * C_QK_DIM
    w_in = w_in.astype(BF16)
    wq = w_in[:, :nqk].reshape(D_MODEL, C_HEADS, C_QK_DIM)[:, :, perm_c].reshape(D_MODEL, nqk)
    wk = w_in[:, nqk:2 * nqk].reshape(D_MODEL, C_HEADS, C_QK_DIM)[:, :, perm_c].reshape(D_MODEL, nqk)
    return dict(wqk=jnp.concatenate([wq, wk], axis=-1).astype(BF16), wvg=w_in[:, 2 * nqk:].astype(BF16),
                lgf=jax.nn.log_sigmoid(decay_f.astype(F32)), lgb=jax.nn.log_sigmoid(decay_b.astype(F32)),
                wout=w_out.astype(BF16))


def _trunk(x, *, batch, seq, tables, norms, even, odd, wup, wdn):
    ca, sa, cb, sb, cc, sc = tables
    g_mix_pre, g_mix_post, g_mlp_pre, g_mlp_post = norms
    for layer in range(DEPTH):
        gpre = g_mix_pre[layer][None]
        gpost = g_mix_post[layer][None]
        if layer % 2 == 0:
            e = even[layer // 2]
            qa, ka, va, qbn, qbr, kbn, vb, kr = _even_proj(
                x, gpre, e["w1"], e["gqa"], e["gka"], e["gcq"], e["gckv"], e["wuq"], e["wukv"],
                ca, sa, cb, sb, seq=seq, tm=512)
            tk, rows_per_step = (seq, 16 * ATTN_ROWS) if seq <= 2048 else (4096, 8 * ATTN_ROWS)
            oa = _attention(qa, ka, va, batch=batch, seq=seq, G=A_HEADS, heads_per_kv=A_HEADS // A_KV_HEADS,
                            tq=rows_per_step // A_HEADS, tk=tk)
            ob = _attention(qbn, kbn, vb, batch=batch, seq=seq, G=B_HEADS // 2, heads_per_kv=1,
                            tq=rows_per_step // (B_HEADS // 2), tk=tk, qr=qbr, kr=kr)
            x = _out_proj(x, gpost, e["wout"], [(oa, 1024, 0), (ob, 1024, 0)], tm=512)
        else:
            o = odd[layer // 2]
            qk = _odd_proj(x, gpre, o["wqk"], tm=1024, seq=seq, tables=(cc, sc))
            vg = _odd_proj(x, gpre, o["wvg"], tm=1024)
            yg = _retention(qk, vg, o["lgf"], o["lgb"], batch=batch, seq=seq, C=256, rows=2048)
            x = _out_proj(x, gpost, o["wout"], [(yg, 4096, 0)], tm=512)
        x = _mlp(x, g_mlp_pre[layer][None], g_mlp_post[layer][None], wup, wdn, layer, tm=1024, tf=512)
    return x


def kernel(x_prompt, x_sample, g_mix_pre, g_mix_post, g_mlp_pre, g_mlp_post, w_in_even, g_qnorm_a, g_knorm_a,
           g_cq_b, w_uq_b, g_ckv_b, w_ukv_b, w_out_even, w_in_odd, decay_fwd, decay_bwd, w_out_odd,
           w_mlp_up, w_mlp_down):
    n_even, n_odd = w_in_even.shape[0], w_in_odd.shape[0]
    even = [_prep_even(w_in_even[e], g_qnorm_a[e], g_knorm_a[e], g_cq_b[e], w_uq_b[e], g_ckv_b[e], w_ukv_b[e],
                       w_out_even[e]) for e in range(n_even)]
    odd = [_prep_odd(w_in_odd[o], decay_fwd[o], decay_bwd[o], w_out_odd[o]) for o in range(n_odd)]
    wup = w_mlp_up.astype(BF16)
    wdn = w_mlp_down.astype(BF16)
    norms = (g_mix_pre, g_mix_post, g_mlp_pre, g_mlp_post)
    tables = _rope_tables(max(x_prompt.shape[1], x_sample.shape[1]))
    outs = []
    for xg in (x_prompt, x_sample):
        b, s, d = xg.shape
        y = _trunk(xg.reshape(b * s, d), batch=b, seq=s, tables=tables, norms=norms, even=even, odd=odd,
                   wup=wup, wdn=wdn)
        outs.append(y.reshape(b, s, d))
    return tuple(outs)
```

```python
import functools

import jax
import jax.numpy as jnp
import numpy as np
from jax import lax
from jax.experimental import pallas as pl
from jax.experimental.pallas import tpu as pltpu

D_MODEL = 2048
DEPTH = 4
GRID_W = 64
ROPE_THETA = 10000.0
NORM_EPS = 1e-6
GN_EPS = 1e-5
A_HEADS, A_KV_HEADS, A_HEAD_DIM = 8, 2, 128
B_HEADS, B_Q_RANK, B_KV_RANK, B_NOPE_DIM, B_ROPE_DIM, B_V_DIM = 8, 512, 256, 128, 64, 128
C_HEADS, C_QK_DIM, C_V_DIM = 8, 256, 512
D_FF = 4 * D_MODEL

VMEM_LIMIT_BYTES = 56 * 1024 * 1024
ATTN_ROWS = 256
LOG2E = 1.4426950408889634

BF16 = jnp.bfloat16
F32 = jnp.float32


def _cparams(semantics):
    return pltpu.CompilerParams(dimension_semantics=semantics, vmem_limit_bytes=VMEM_LIMIT_BYTES)


def _rms(x, g):
    return x * lax.rsqrt(jnp.mean(x * x, axis=-1, keepdims=True) + NORM_EPS) * g


def _dot(a, b):
    return jnp.dot(a, b, preferred_element_type=F32)


def _dot_nt(a, b):
    return lax.dot_general(a, b, (((1,), (1,)), ((), ())), preferred_element_type=F32)


def _dot_tn(a, b):
    return lax.dot_general(a, b, (((0,), (0,)), ((), ())), preferred_element_type=F32)


def _rope_half(x, c, s):
    return x * c + pltpu.roll(x, 64, 1) * s


EVEN_W1_COLS = 1024 + 256 + 256 + 512 + 256 + 128


def _even_proj_kernel(x_ref, gpre_ref, w1_ref, gqa_ref, gka_ref, gcq_ref, gckv_ref, wuq_ref, wukv_ref,
                      ca_ref, sa_ref, cb_ref, sb_ref,
                      qa_ref, ka_ref, va_ref, qbn_ref, qbr_ref, kbn_ref, vb_ref, kr_ref):
    hn = _rms(x_ref[...], gpre_ref[...]).astype(BF16)
    p = _dot(hn, w1_ref[...])
    ca, sa = ca_ref[...], sa_ref[...]
    cb, sb = cb_ref[...], sb_ref[...]
    gq = gqa_ref[...] * (A_HEAD_DIM ** -0.5 * LOG2E)
    gk = gka_ref[...]
    for h in range(A_HEADS):
        xh = p[:, h * 128:(h + 1) * 128]
        qa_ref[:, h * 128:(h + 1) * 128] = _rope_half(_rms(xh, gq), ca, sa).astype(BF16)
    for h in range(A_KV_HEADS):
        xh = p[:, 1024 + h * 128:1024 + (h + 1) * 128]
        ka_ref[:, h * 128:(h + 1) * 128] = _rope_half(_rms(xh, gk), ca, sa).astype(BF16)
    va_ref[...] = p[:, 1280:1536].astype(BF16)
    cqn = _rms(p[:, 1536:2048], gcq_ref[...]).astype(BF16)
    qb = _dot(cqn, wuq_ref[...]) * ((B_NOPE_DIM + B_ROPE_DIM) ** -0.5 * LOG2E)
    qbn_ref[...] = qb[:, :1024].astype(BF16)
    for h in range(B_HEADS):
        xh = qb[:, 1024 + h * 128:1024 + (h + 1) * 128]
        qbr_ref[:, h * 128:(h + 1) * 128] = _rope_half(xh, cb, sb).astype(BF16)
    ckvn = _rms(p[:, 2048:2304], gckv_ref[...]).astype(BF16)
    kv = _dot(ckvn, wukv_ref[...])
    kbn_ref[...] = kv[:, :1024].astype(BF16)
    vb_ref[...] = kv[:, 1024:].astype(BF16)
    kr_ref[...] = _rope_half(p[:, 2304:2432], cb, sb).astype(BF16)


def _even_proj(x, gpre, w1, gqa, gka, gcq, gckv, wuq, wukv, ca, sa, cb, sb, *, seq, tm):
    T = x.shape[0]
    nblk_seq = seq // tm
    row = lambda i: (i, 0)
    fixed = lambda i: (0, 0)
    pos = lambda i: (i % nblk_seq, 0)
    out_cols = (1024, 256, 256, 1024, 1024, 1024, 1024, 128)
    return pl.pallas_call(
        _even_proj_kernel,
        out_shape=[jax.ShapeDtypeStruct((T, c), BF16) for c in out_cols],
        grid=(T // tm,),
        in_specs=[
            pl.BlockSpec((tm, D_MODEL), row),
            pl.BlockSpec((1, D_MODEL), fixed),
            pl.BlockSpec((D_MODEL, EVEN_W1_COLS), fixed, pipeline_mode=pl.Buffered(1)),
            pl.BlockSpec((1, 128), fixed),
            pl.BlockSpec((1, 128), fixed),
            pl.BlockSpec((1, B_Q_RANK), fixed),
            pl.BlockSpec((1, B_KV_RANK), fixed),
            pl.BlockSpec((B_Q_RANK, 2048), fixed, pipeline_mode=pl.Buffered(1)),
            pl.BlockSpec((B_KV_RANK, 2048), fixed, pipeline_mode=pl.Buffered(1)),
            pl.BlockSpec((tm, 128), pos),
            pl.BlockSpec((tm, 128), pos),
            pl.BlockSpec((tm, 128), pos),
            pl.BlockSpec((tm, 128), pos),
        ],
        out_specs=[pl.BlockSpec((tm, c), row) for c in out_cols],
        compiler_params=_cparams(("arbitrary",)),
        name="even_proj",
    )(x, gpre, w1, gqa, gka, gcq, gckv, wuq, wukv, ca, sa, cb, sb)


def _attn_kernel(*refs, G, heads_per_kv, rope_ext, nk):
    if rope_ext:
        q_ref, qr_ref, k_ref, kr_ref, v_ref, o_ref = refs[:6]
        scratch = refs[6:]
    else:
        q_ref, k_ref, v_ref, o_ref = refs[:4]
        qr_ref = kr_ref = None
        scratch = refs[4:]
    ki = pl.program_id(3)
    tq = q_ref.shape[0]
    items = [(g, slice(r, r + ATTN_ROWS)) for g in range(G) for r in range(0, tq, ATTN_ROWS)]

    def scores(g, rows):
        sl = slice(g * 128, (g + 1) * 128)
        kvh = g // heads_per_kv
        q, k = q_ref[rows, sl], k_ref[:, kvh * 128:(kvh + 1) * 128]
        if rope_ext:
            q = jnp.concatenate([q, qr_ref[rows, sl]], axis=-1)
            k = jnp.concatenate([k, kr_ref[...]], axis=-1)
        return _dot_nt(q, k)

    n_kv = G // heads_per_kv
    ones = jnp.ones((v_ref.shape[0], 128), BF16)

    def values_and_ones(negate):
        ext = [jnp.concatenate([v_ref[:, h * 128:(h + 1) * 128], ones], axis=-1) for h in range(n_kv)]
        return [-e for e in ext] if negate else ext

    def normalised(o_and_sum):
        return (o_and_sum[:, :128] / o_and_sum[:, 128:]).astype(o_ref.dtype)

    if nk == 1:
        v_ext = values_and_ones(negate=False)
        for g, rows in items:
            s = scores(g, rows)
            p = jnp.exp2(s - jnp.max(s, axis=-1, keepdims=True))
            o_ref[rows, g * 128:(g + 1) * 128] = normalised(_dot(p.astype(BF16), v_ext[g // heads_per_kv]))
        return

    m_sc, acc_sc = scratch

    @pl.when(ki == 0)
    def _():
        m_sc[...] = jnp.full(m_sc.shape, -jnp.inf, F32)
        acc_sc[...] = jnp.zeros(acc_sc.shape, F32)

    neg_v_ext = values_and_ones(negate=True)
    for g, rows in items:
        s = scores(g, rows)
        m_prev = m_sc[g, rows]
        m_new = jnp.maximum(m_prev, jnp.max(s, axis=-1, keepdims=True))
        alpha = jnp.exp2(m_prev - m_new)
        p = jnp.exp2(s - m_new)
        acc_sc[g, rows] = alpha * acc_sc[g, rows] - _dot(p.astype(BF16), neg_v_ext[g // heads_per_kv])
        m_sc[g, rows] = m_new

    @pl.when(ki == nk - 1)
    def _():
        for g in range(G):
            o_ref[:, g * 128:(g + 1) * 128] = normalised(acc_sc[g])


def _attention(q, k, v, *, batch, seq, G, heads_per_kv, tq, tk, qr=None, kr=None):
    T, qw = q.shape
    n_groups = qw // (G * 128)
    kvw = G // heads_per_kv * 128
    assert kvw % 256 == 0
    nq, nk = seq // tq, seq // tk
    rope_ext = qr is not None
    q_map = lambda b, h, qi, ki: (b * nq + qi, h)
    kv_map = lambda b, h, qi, ki: (b * nk + ki, h)
    kr_map = lambda b, h, qi, ki: (b * nk + ki, 0)
    in_specs = [pl.BlockSpec((tq, G * 128), q_map)]
    args = [q]
    if rope_ext:
        in_specs.append(pl.BlockSpec((tq, G * 128), q_map))
        args.append(qr)
    in_specs.append(pl.BlockSpec((tk, kvw), kv_map))
    args.append(k)
    if rope_ext:
        in_specs.append(pl.BlockSpec((tk, 128), kr_map))
        args.append(kr)
    in_specs.append(pl.BlockSpec((tk, kvw), kv_map))
    args.append(v)
    scratch = []
    if nk > 1:
        scratch = [pltpu.VMEM((G, tq, 1), F32), pltpu.VMEM((G, tq, 256), F32)]
    return pl.pallas_call(
        functools.partial(_attn_kernel, G=G, heads_per_kv=heads_per_kv, rope_ext=rope_ext, nk=nk),
        out_shape=jax.ShapeDtypeStruct((T, qw), BF16),
        grid=(batch, n_groups, nq, nk),
        in_specs=in_specs,
        out_specs=pl.BlockSpec((tq, G * 128), q_map),
        scratch_shapes=scratch,
        compiler_params=_cparams(("arbitrary",) * 4),
        name="attn_mla" if rope_ext else "attn_gqa",
    )(*args)


def _out_proj_kernel(*refs, widths):
    x_ref, g_ref, w_ref = refs[:3]
    a_refs = refs[3:3 + len(widths)]
    o_ref = refs[3 + len(widths)]
    acc = None
    off = 0
    for a_ref, wd in zip(a_refs, widths):
        part = _dot(a_ref[...], w_ref[off:off + wd, :])
        acc = part if acc is None else acc + part
        off += wd
    o_ref[...] = x_ref[...] + _rms(acc, g_ref[...])


def _out_proj(x, g, w, parts, *, tm):
    T = x.shape[0]
    widths = tuple(wd for _, wd, _ in parts)
    row = lambda i: (i, 0)
    return pl.pallas_call(
        functools.partial(_out_proj_kernel, widths=widths),
        out_shape=jax.ShapeDtypeStruct((T, D_MODEL), F32),
        grid=(T // tm,),
        in_specs=[
            pl.BlockSpec((tm, D_MODEL), row),
            pl.BlockSpec((1, D_MODEL), lambda i: (0, 0)),
            pl.BlockSpec(w.shape, lambda i: (0, 0), pipeline_mode=pl.Buffered(1)),
        ] + [pl.BlockSpec((tm, wd), functools.partial(lambda i, cb: (i, cb), cb=cb)) for _, wd, cb in parts],
        out_specs=pl.BlockSpec((tm, D_MODEL), row),
        compiler_params=_cparams(("arbitrary",)),
        name="out_proj",
    )(x, g, w, *[a for a, _, _ in parts])


def _mlp_kernel(x_ref, gpre_ref, gpost_ref, wup_ref, wdn_ref, o_ref, hn_sc, *, nf):
    f = pl.program_id(1)

    @pl.when(f == 0)
    def _():
        hn_sc[...] = _rms(x_ref[...], gpre_ref[...]).astype(BF16)
        o_ref[...] = jnp.zeros(o_ref.shape, F32)

    h = jnp.maximum(_dot(hn_sc[...], wup_ref[...]), 0.0)
    o_ref[...] += _dot((h * h).astype(BF16), wdn_ref[...])

    @pl.when(f == nf - 1)
    def _():
        o_ref[...] = x_ref[...] + _rms(o_ref[...], gpost_ref[...])


def _mlp(x, gpre, gpost, wup, wdn, layer, *, tm, tf):
    T = x.shape[0]
    nf = D_FF // tf
    row = lambda i, f: (i, 0)
    fixed = lambda i, f: (0, 0)
    return pl.pallas_call(
        functools.partial(_mlp_kernel, nf=nf),
        out_shape=jax.ShapeDtypeStruct((T, D_MODEL), F32),
        grid=(T // tm, nf),
        in_specs=[
            pl.BlockSpec((tm, D_MODEL), row),
            pl.BlockSpec((1, D_MODEL), fixed),
            pl.BlockSpec((1, D_MODEL), fixed),
            pl.BlockSpec((None, D_MODEL, tf), lambda i, f: (layer, 0, f)),
            pl.BlockSpec((None, tf, D_MODEL), lambda i, f: (layer, f, 0)),
        ],
        out_specs=pl.BlockSpec((tm, D_MODEL), row),
        scratch_shapes=[pltpu.VMEM((tm, D_MODEL), BF16)],
        compiler_params=_cparams(("arbitrary", "arbitrary")),
        name="mlp",
    )(x, gpre, gpost, wup, wdn)


ODD_TN = 1024


def _odd_proj_kernel(x_ref, gpre_ref, w_ref, *rest, rope, n_q_blocks):
    if rope:
        c_ref, s_ref, o_ref, hn_sc = rest
    else:
        o_ref, hn_sc = rest
    j = pl.program_id(1)

    @pl.when(j == 0)
    def _():
        hn_sc[...] = _rms(x_ref[...], gpre_ref[...]).astype(BF16)

    acc = _dot(hn_sc[...], w_ref[...])
    if not rope:
        o_ref[...] = acc.astype(BF16)
        return
    scale = jnp.where(j < n_q_blocks, 1.0, C_QK_DIM ** -0.5).astype(F32)
    c = c_ref[...] * scale
    s = s_ref[...] * scale
    for h in range(ODD_TN // C_QK_DIM):
        x0 = acc[:, h * 256:h * 256 + 128]
        x1 = acc[:, h * 256 + 128:(h + 1) * 256]
        o_ref[:, h * 256:h * 256 + 128] = (x0 * c - x1 * s).astype(BF16)
        o_ref[:, h * 256 + 128:(h + 1) * 256] = (x0 * s + x1 * c).astype(BF16)


def _odd_proj(x, gpre, w, *, tm, seq=None, tables=None):
    T = x.shape[0]
    N = w.shape[1]
    rope = tables is not None
    in_specs = [
        pl.BlockSpec((tm, D_MODEL), lambda i, j: (i, 0)),
        pl.BlockSpec((1, D_MODEL), lambda i, j: (0, 0)),
        pl.BlockSpec((D_MODEL, ODD_TN), lambda i, j: (0, j)),
    ]
    args = [x, gpre, w]
    if rope:
        nblk_seq = seq // tm
        in_specs += [pl.BlockSpec((tm, 128), lambda i, j: (i % nblk_seq, 0))] * 2
        args += list(tables)
    return pl.pallas_call(
        functools.partial(_odd_proj_kernel, rope=rope, n_q_blocks=N // ODD_TN // 2),
        out_shape=jax.ShapeDtypeStruct((T, N), BF16),
        grid=(T // tm, N // ODD_TN),
        in_specs=in_specs,
        out_specs=pl.BlockSpec((tm, ODD_TN), lambda i, j: (i, j)),
        scratch_shapes=[pltpu.VMEM((tm, D_MODEL), BF16)],
        compiler_params=_cparams(("arbitrary", "arbitrary")),
        name="odd_qk_proj" if rope else "odd_vg_proj",
    )(*args)


def _ret_fwd_kernel(lgf_ref, lgb_ref, q_ref, k_ref, v_ref, y_ref, st_sc, d_sc, qd_sc, kd_sc, cd_sc, *, C, n_chunks):
    h = pl.program_id(1)
    lgf = lgf_ref[h]
    lgb = lgb_ref[h]

    @pl.when(pl.program_id(2) == 0)
    def _():
        st_sc[...] = jnp.zeros(st_sc.shape, F32)
        cd_sc[...] = jnp.exp(jnp.full(cd_sc.shape, C, F32) * lgf)
        i = lax.broadcasted_iota(jnp.int32, (C, C), 0)
        j = lax.broadcasted_iota(jnp.int32, (C, C), 1)
        rel = (i - j).astype(F32)
        d_sc[...] = jnp.where(rel >= 0, jnp.exp(rel * lgf), jnp.exp(-rel * lgb))
        idx = lax.broadcasted_iota(jnp.int32, (C, C_QK_DIM), 0).astype(F32)
        qd_sc[...] = jnp.exp((idx + 1.0) * lgf)
        kd_sc[...] = jnp.exp((C - 1.0 - idx) * lgf)

    st = st_sc[...]
    for c in range(n_chunks):
        rows = slice(c * C, (c + 1) * C)
        q, k, v = q_ref[rows, :], k_ref[rows, :], v_ref[rows, :]
        a = (_dot_nt(q, k) * d_sc[...]).astype(BF16)
        qd = (q.astype(F32) * qd_sc[...]).astype(BF16)
        y_ref[rows, :] = (_dot(a, v) + _dot(qd, st.astype(BF16))).astype(y_ref.dtype)
        kd = (k.astype(F32) * kd_sc[...]).astype(BF16)
        st = st * cd_sc[...] + _dot_tn(kd, v)
    st_sc[...] = st


def _ret_bwd_kernel(lgb_ref, q_ref, k_ref, v_ref, g_ref, yp_ref, o_ref, st_sc, qd_sc, kd_sc, cd_sc, *, C, n_chunks):
    h = pl.program_id(1)
    lgb = lgb_ref[h]

    @pl.when(pl.program_id(2) == 0)
    def _():
        st_sc[...] = jnp.zeros(st_sc.shape, F32)
        cd_sc[...] = jnp.exp(jnp.full(cd_sc.shape, C, F32) * lgb)
        idx = lax.broadcasted_iota(jnp.int32, (C, C_QK_DIM), 0).astype(F32)
        qd_sc[...] = jnp.exp((C - idx) * lgb)
        kd_sc[...] = jnp.exp(idx * lgb)

    st = st_sc[...]
    for c in reversed(range(n_chunks)):
        rows = slice(c * C, (c + 1) * C)
        q, k, v = q_ref[rows, :], k_ref[rows, :], v_ref[rows, :]
        qd = (q.astype(F32) * qd_sc[...]).astype(BF16)
        y = yp_ref[rows, :].astype(F32) + _dot(qd, st.astype(BF16))
        kd = (k.astype(F32) * kd_sc[...]).astype(BF16)
        st = st * cd_sc[...] + _dot_tn(kd, v)
        mu = jnp.mean(y, axis=-1, keepdims=True)
        yc = y - mu
        var = jnp.mean(yc * yc, axis=-1, keepdims=True)
        yn = yc * lax.rsqrt(var + GN_EPS)
        g = g_ref[rows, :].astype(F32)
        o_ref[rows, :] = (g * jax.nn.sigmoid(g) * yn).astype(BF16)
    st_sc[...] = st


def _retention(qk, vg, lgf, lgb, *, batch, seq, C, rows):
    T = qk.shape[0]
    nb = seq // rows
    n_chunks = rows // C
    H = C_HEADS
    fwd = lambda b, h, r: b * nb + r
    bwd = lambda b, h, r: b * nb + (nb - 1 - r)
    qk_blk = (rows, C_QK_DIM)
    v_blk = (rows, C_V_DIM)
    state_scratch = [pltpu.VMEM((C_QK_DIM, C_V_DIM), F32)]
    dec_scratch = [pltpu.VMEM((C, C_QK_DIM), F32), pltpu.VMEM((C, C_QK_DIM), F32), pltpu.VMEM((1, C_V_DIM), F32)]
    y_part = pl.pallas_call(
        functools.partial(_ret_fwd_kernel, C=C, n_chunks=n_chunks),
        out_shape=jax.ShapeDtypeStruct((T, H * C_V_DIM), BF16),
        grid_spec=pltpu.PrefetchScalarGridSpec(
            num_scalar_prefetch=2,
            grid=(batch, H, nb),
            in_specs=[
                pl.BlockSpec(qk_blk, lambda b, h, r, *_: (fwd(b, h, r), h)),
                pl.BlockSpec(qk_blk, lambda b, h, r, *_: (fwd(b, h, r), H + h)),
                pl.BlockSpec(v_blk, lambda b, h, r, *_: (fwd(b, h, r), h)),
            ],
            out_specs=pl.BlockSpec(v_blk, lambda b, h, r, *_: (fwd(b, h, r), h)),
            scratch_shapes=state_scratch + [pltpu.VMEM((C, C), F32)] + dec_scratch,
        ),
        compiler_params=_cparams(("arbitrary",) * 3),
        name="ret_fwd",
    )(lgf, lgb, qk, qk, vg)
    return pl.pallas_call(
        functools.partial(_ret_bwd_kernel, C=C, n_chunks=n_chunks),
        out_shape=jax.ShapeDtypeStruct((T, H * C_V_DIM), BF16),
        grid_spec=pltpu.PrefetchScalarGridSpec(
            num_scalar_prefetch=1,
            grid=(batch, H, nb),
            in_specs=[
                pl.BlockSpec(qk_blk, lambda b, h, r, *_: (bwd(b, h, r), h)),
                pl.BlockSpec(qk_blk, lambda b, h, r, *_: (bwd(b, h, r), H + h)),
                pl.BlockSpec(v_blk, lambda b, h, r, *_: (bwd(b, h, r), h)),
                pl.BlockSpec(v_blk, lambda b, h, r, *_: (bwd(b, h, r), H + h)),
                pl.BlockSpec(v_blk, lambda b, h, r, *_: (bwd(b, h, r), h)),
            ],
            out_specs=pl.BlockSpec(v_blk, lambda b, h, r, *_: (bwd(b, h, r), h)),
            scratch_shapes=state_scratch + dec_scratch,
        ),
        compiler_params=_cparams(("arbitrary",) * 3),
        name="ret_bwd",
    )(lgb, qk, qk, vg, vg, y_part)


def _deinterleave(d):
    return np.concatenate([np.arange(0, d, 2), np.arange(1, d, 2)])


def _rope_tables(seq_max):
    t = np.arange(seq_max)
    row = (t // GRID_W).astype(np.float32)
    col = (t % GRID_W).astype(np.float32)

    def ang(d_rot):
        n = d_rot // 4
        inv = jnp.asarray(ROPE_THETA, F32) ** (-jnp.arange(n, dtype=F32) / n)
        return jnp.concatenate([jnp.asarray(row)[:, None] * inv, jnp.asarray(col)[:, None] * inv], axis=-1)

    a = ang(A_HEAD_DIM)
    ca = jnp.concatenate([jnp.cos(a), jnp.cos(a)], axis=-1)
    sa = jnp.concatenate([-jnp.sin(a), jnp.sin(a)], axis=-1)
    b = ang(B_ROPE_DIM)
    z = jnp.zeros_like(b)
    cb = jnp.concatenate([jnp.cos(b), z, jnp.cos(b), z], axis=-1)
    sb = jnp.concatenate([-jnp.sin(b), z, jnp.sin(b), z], axis=-1)
    c = ang(C_QK_DIM)
    return ca, sa, cb, sb, jnp.cos(c), jnp.sin(c)


def _prep_even(w_in, g_qa, g_ka, g_cq, w_uq, g_ckv, w_ukv, w_out):
    perm_a = _deinterleave(A_HEAD_DIM)
    perm_b = _deinterleave(B_ROPE_DIM)
    w_in, w_uq, w_ukv = w_in.astype(BF16), w_uq.astype(BF16), w_ukv.astype(BF16)
    o = 0
    wqa = w_in[:, o:o + 1024].reshape(D_MODEL, A_HEADS, 128)[:, :, perm_a].reshape(D_MODEL, 1024)
    o += 1024
    wka = w_in[:, o:o + 256].reshape(D_MODEL, A_KV_HEADS, 128)[:, :, perm_a].reshape(D_MODEL, 256)
    o += 256
    wva = w_in[:, o:o + 256]
    o += 256
    wcq = w_in[:, o:o + 512]
    o += 512
    wckv = w_in[:, o:o + 256]
    o += 256
    wkr = w_in[:, o:o + 64][:, perm_b]
    z32 = jnp.zeros((D_MODEL, 32), w_in.dtype)
    wkr = jnp.concatenate([wkr[:, :32], z32, wkr[:, 32:], z32], axis=-1)
    w1 = jnp.concatenate([wqa, wka, wva, wcq, wckv, wkr], axis=-1).astype(BF16)
    uq = w_uq.reshape(B_Q_RANK, B_HEADS, B_NOPE_DIM + B_ROPE_DIM)
    uq_nope = uq[:, :, :B_NOPE_DIM].reshape(B_Q_RANK, 1024)
    uq_rope = uq[:, :, B_NOPE_DIM:][:, :, perm_b]
    zr = jnp.zeros((B_Q_RANK, B_HEADS, 32), w_uq.dtype)
    uq_rope = jnp.concatenate([uq_rope[:, :, :32], zr, uq_rope[:, :, 32:], zr], axis=-1).reshape(B_Q_RANK, 1024)
    wuq = jnp.concatenate([uq_nope, uq_rope], axis=-1).astype(BF16)
    ukv = w_ukv.reshape(B_KV_RANK, B_HEADS, B_NOPE_DIM + B_V_DIM)
    wukv = jnp.concatenate([ukv[:, :, :B_NOPE_DIM].reshape(B_KV_RANK, 1024),
                            ukv[:, :, B_NOPE_DIM:].reshape(B_KV_RANK, 1024)], axis=-1).astype(BF16)
    return dict(w1=w1, gqa=g_qa[perm_a][None], gka=g_ka[perm_a][None], gcq=g_cq[None], gckv=g_ckv[None],
                wuq=wuq, wukv=wukv, wout=w_out.astype(BF16))


def _prep_odd(w_in, decay_f, decay_b, w_out):
    perm_c = _deinterleave(C_QK_DIM)
    nqk = C_HEADS * C_QK_DIM
    w_in = w_in.astype(BF16)
    wq = w_in[:, :nqk].reshape(D_MODEL, C_HEADS, C_QK_DIM)[:, :, perm_c].reshape(D_MODEL, nqk)
    wk = w_in[:, nqk:2 * nqk].reshape(D_MODEL, C_HEADS, C_QK_DIM)[:, :, perm_c].reshape(D_MODEL, nqk)
    return dict(wqk=jnp.concatenate([wq, wk], axis=-1).astype(BF16), wvg=w_in[:, 2 * nqk:].astype(BF16),
                lgf=jax.nn.log_sigmoid(decay_f.astype(F32)), lgb=jax.nn.log_sigmoid(decay_b.astype(F32)),
                wout=w_out.astype(BF16))


def _trunk(x, *, batch, seq, tables, norms, even, odd, wup, wdn):
    ca, sa, cb, sb, cc, sc = tables
    g_mix_pre, g_mix_post, g_mlp_pre, g_mlp_post = norms
    for layer in range(DEPTH):
        gpre = g_mix_pre[layer][None]
        gpost = g_mix_post[layer][None]
        if layer % 2 == 0:
            e = even[layer // 2]
            qa, ka, va, qbn, qbr, kbn, vb, kr = _even_proj(
                x, gpre, e["w1"], e["gqa"], e["gka"], e["gcq"], e["gckv"], e["wuq"], e["wukv"],
                ca, sa, cb, sb, seq=seq, tm=512)
            tk, rows_per_step = (seq, 16 * ATTN_ROWS) if seq <= 2048 else (4096, 8 * ATTN_ROWS)
            oa = _attention(qa, ka, va, batch=batch, seq=seq, G=A_HEADS, heads_per_kv=A_HEADS // A_KV_HEADS,
                            tq=rows_per_step // A_HEADS, tk=tk)
            ob = _attention(qbn, kbn, vb, batch=batch, seq=seq, G=B_HEADS // 2, heads_per_kv=1,
                            tq=rows_per_step // (B_HEADS // 2), tk=tk, qr=qbr, kr=kr)
            x = _out_proj(x, gpost, e["wout"], [(oa, 1024, 0), (ob, 1024, 0)], tm=512)
        else:
            o = odd[layer // 2]
            qk = _odd_proj(x, gpre, o["wqk"], tm=1024, seq=seq, tables=(cc, sc))
            vg = _odd_proj(x, gpre, o["wvg"], tm=1024)
            yg = _retention(qk, vg, o["lgf"], o["lgb"], batch=batch, seq=seq, C=256, rows=2048)
            x = _out_proj(x, gpost, o["wout"], [(yg, 4096, 0)], tm=512)
        x = _mlp(x, g_mlp_pre[layer][None], g_mlp_post[layer][None], wup, wdn, layer, tm=1024, tf=512)
    return x


def kernel(x_prompt, x_sample, g_mix_pre, g_mix_post, g_mlp_pre, g_mlp_post, w_in_even, g_qnorm_a, g_knorm_a,
           g_cq_b, w_uq_b, g_ckv_b, w_ukv_b, w_out_even, w_in_odd, decay_fwd, decay_bwd, w_out_odd,
           w_mlp_up, w_mlp_down):
    n_even, n_odd = w_in_even.shape[0], w_in_odd.shape[0]
    even = [_prep_even(w_in_even[e], g_qnorm_a[e], g_knorm_a[e], g_cq_b[e], w_uq_b[e], g_ckv_b[e], w_ukv_b[e],
                       w_out_even[e]) for e in range(n_even)]
    odd = [_prep_odd(w_in_odd[o], decay_fwd[o], decay_bwd[o], w_out_odd[o]) for o in range(n_odd)]
    wup = w_mlp_up.astype(BF16)
    wdn = w_mlp_down.astype(BF16)
    norms = (g_mix_pre, g_mix_post, g_mlp_pre, g_mlp_post)
    tables = _rope_tables(max(x_prompt.shape[1], x_sample.shape[1]))
    outs = []
    for xg in (x_prompt, x_sample):
        b, s, d = xg.shape
        y = _trunk(xg.reshape(b * s, d), batch=b, seq=s, tables=tables, norms=norms, even=even, odd=odd,
                   wup=wup, wdn=wdn)
        outs.append(y.reshape(b, s, d))
    return tuple(outs)
```

```python
import functools

import jax
import jax.numpy as jnp
import numpy as np
from jax import lax
from jax.experimental import pallas as pl
from jax.experimental.pallas import tpu as pltpu

D_MODEL = 2048
DEPTH = 4
GRID_W = 64
ROPE_THETA = 10000.0
NORM_EPS = 1e-6
GN_EPS = 1e-5
A_HEADS, A_KV_HEADS, A_HEAD_DIM = 8, 2, 128
B_HEADS, B_Q_RANK, B_KV_RANK, B_NOPE_DIM, B_ROPE_DIM, B_V_DIM = 8, 512, 256, 128, 64, 128
C_HEADS, C_QK_DIM, C_V_DIM = 8, 256, 512
D_FF = 4 * D_MODEL

VMEM_LIMIT_BYTES = 56 * 1024 * 1024
ATTN_ROWS = 256
LOG2E = 1.4426950408889634

BF16 = jnp.bfloat16
F32 = jnp.float32


def _cparams(semantics):
    return pltpu.CompilerParams(dimension_semantics=semantics, vmem_limit_bytes=VMEM_LIMIT_BYTES)


def _rms(x, g):
    return x * lax.rsqrt(jnp.mean(x * x, axis=-1, keepdims=True) + NORM_EPS) * g


def _dot(a, b):
    return jnp.dot(a, b, preferred_element_type=F32)


def _dot_nt(a, b):
    return lax.dot_general(a, b, (((1,), (1,)), ((), ())), preferred_element_type=F32)


def _dot_tn(a, b):
    return lax.dot_general(a, b, (((0,), (0,)), ((), ())), preferred_element_type=F32)


def _rope_half(x, c, s):
    return x * c + pltpu.roll(x, 64, 1) * s


EVEN_W1_COLS = 1024 + 256 + 256 + 512 + 256 + 128


def _even_proj_kernel(x_ref, gpre_ref, w1_ref, gqa_ref, gka_ref, gcq_ref, gckv_ref, wuq_ref, wukv_ref,
                      ca_ref, sa_ref, cb_ref, sb_ref,
                      qa_ref, ka_ref, va_ref, qbn_ref, qbr_ref, kbn_ref, vb_ref, kr_ref):
    hn = _rms(x_ref[...], gpre_ref[...]).astype(BF16)
    p = _dot(hn, w1_ref[...])
    ca, sa = ca_ref[...], sa_ref[...]
    cb, sb = cb_ref[...], sb_ref[...]
    gq = gqa_ref[...] * (A_HEAD_DIM ** -0.5 * LOG2E)
    gk = gka_ref[...]
    for h in range(A_HEADS):
        xh = p[:, h * 128:(h + 1) * 128]
        qa_ref[:, h * 128:(h + 1) * 128] = _rope_half(_rms(xh, gq), ca, sa).astype(BF16)
    for h in range(A_KV_HEADS):
        xh = p[:, 1024 + h * 128:1024 + (h + 1) * 128]
        ka_ref[:, h * 128:(h + 1) * 128] = _rope_half(_rms(xh, gk), ca, sa).astype(BF16)
    va_ref[...] = p[:, 1280:1536].astype(BF16)
    cqn = _rms(p[:, 1536:2048], gcq_ref[...]).astype(BF16)
    qb = _dot(cqn, wuq_ref[...]) * ((B_NOPE_DIM + B_ROPE_DIM) ** -0.5 * LOG2E)
    qbn_ref[...] = qb[:, :1024].astype(BF16)
    for h in range(B_HEADS):
        xh = qb[:, 1024 + h * 128:1024 + (h + 1) * 128]
        qbr_ref[:, h * 128:(h + 1) * 128] = _rope_half(xh, cb, sb).astype(BF16)
    ckvn = _rms(p[:, 2048:2304], gckv_ref[...]).astype(BF16)
    kv = _dot(ckvn, wukv_ref[...])
    kbn_ref[...] = kv[:, :1024].astype(BF16)
    vb_ref[...] = kv[:, 1024:].astype(BF16)
    kr_ref[...] = _rope_half(p[:, 2304:2432], cb, sb).astype(BF16)


def _even_proj(x, gpre, w1, gqa, gka, gcq, gckv, wuq, wukv, ca, sa, cb, sb, *, seq, tm):
    T = x.shape[0]
    nblk_seq = seq // tm
    row = lambda i: (i, 0)
    fixed = lambda i: (0, 0)
    pos = lambda i: (i % nblk_seq, 0)
    out_cols = (1024, 256, 256, 1024, 1024, 1024, 1024, 128)
    return pl.pallas_call(
        _even_proj_kernel,
        out_shape=[jax.ShapeDtypeStruct((T, c), BF16) for c in out_cols],
        grid=(T // tm,),
        in_specs=[
            pl.BlockSpec((tm, D_MODEL), row),
            pl.BlockSpec((1, D_MODEL), fixed),
            pl.BlockSpec((D_MODEL, EVEN_W1_COLS), fixed, pipeline_mode=pl.Buffered(1)),
            pl.BlockSpec((1, 128), fixed),
            pl.BlockSpec((1, 128), fixed),
            pl.BlockSpec((1, B_Q_RANK), fixed),
            pl.BlockSpec((1, B_KV_RANK), fixed),
            pl.BlockSpec((B_Q_RANK, 2048), fixed, pipeline_mode=pl.Buffered(1)),
            pl.BlockSpec((B_KV_RANK, 2048), fixed, pipeline_mode=pl.Buffered(1)),
            pl.BlockSpec((tm, 128), pos),
            pl.BlockSpec((tm, 128), pos),
            pl.BlockSpec((tm, 128), pos),
            pl.BlockSpec((tm, 128), pos),
        ],
        out_specs=[pl.BlockSpec((tm, c), row) for c in out_cols],
        compiler_params=_cparams(("arbitrary",)),
        name="even_proj",
    )(x, gpre, w1, gqa, gka, gcq, gckv, wuq, wukv, ca, sa, cb, sb)


def _attn_kernel(*refs, G, heads_per_kv, rope_ext, nk):
    if rope_ext:
        q_ref, qr_ref, k_ref, kr_ref, v_ref, o_ref = refs[:6]
        scratch = refs[6:]
    else:
        q_ref, k_ref, v_ref, o_ref = refs[:4]
        qr_ref = kr_ref = None
        scratch = refs[4:]
    ki = pl.program_id(3)
    tq = q_ref.shape[0]
    items = [(g, slice(r, r + ATTN_ROWS)) for g in range(G) for r in range(0, tq, ATTN_ROWS)]

    def scores(g, rows):
        sl = slice(g * 128, (g + 1) * 128)
        kvh = g // heads_per_kv
        q, k = q_ref[rows, sl], k_ref[:, kvh * 128:(kvh + 1) * 128]
        if rope_ext:
            q = jnp.concatenate([q, qr_ref[rows, sl]], axis=-1)
            k = jnp.concatenate([k, kr_ref[...]], axis=-1)
        return _dot_nt(q, k)

    n_kv = G // heads_per_kv
    ones = jnp.ones((v_ref.shape[0], 128), BF16)

    def values_and_ones(negate):
        ext = [jnp.concatenate([v_ref[:, h * 128:(h + 1) * 128], ones], axis=-1) for h in range(n_kv)]
        return [-e for e in ext] if negate else ext

    def normalised(o_and_sum):
        return (o_and_sum[:, :128] / o_and_sum[:, 128:]).astype(o_ref.dtype)

    if nk == 1:
        v_ext = values_and_ones(negate=False)
        for g, rows in items:
            s = scores(g, rows)
            p = jnp.exp2(s - jnp.max(s, axis=-1, keepdims=True))
            o_ref[rows, g * 128:(g + 1) * 128] = normalised(_dot(p.astype(BF16), v_ext[g // heads_per_kv]))
        return

    m_sc, acc_sc = scratch

    @pl.when(ki == 0)
    def _():
        m_sc[...] = jnp.full(m_sc.shape, -jnp.inf, F32)
        acc_sc[...] = jnp.zeros(acc_sc.shape, F32)

    neg_v_ext = values_and_ones(negate=True)
    for g, rows in items:
        s = scores(g, rows)
        m_prev = m_sc[g, rows]
        m_new = jnp.maximum(m_prev, jnp.max(s, axis=-1, keepdims=True))
        alpha = jnp.exp2(m_prev - m_new)
        p = jnp.exp2(s - m_new)
        acc_sc[g, rows] = alpha * acc_sc[g, rows] - _dot(p.astype(BF16), neg_v_ext[g // heads_per_kv])
        m_sc[g, rows] = m_new

    @pl.when(ki == nk - 1)
    def _():
        for g in range(G):
            o_ref[:, g * 128:(g + 1) * 128] = normalised(acc_sc[g])


def _attention(q, k, v, *, batch, seq, G, heads_per_kv, tq, tk, qr=None, kr=None):
    T, qw = q.shape
    n_groups = qw // (G * 128)
    kvw = G // heads_per_kv * 128
    assert kvw % 256 == 0
    nq, nk = seq // tq, seq // tk
    rope_ext = qr is not None
    q_map = lambda b, h, qi, ki: (b * nq + qi, h)
    kv_map = lambda b, h, qi, ki: (b * nk + ki, h)
    kr_map = lambda b, h, qi, ki: (b * nk + ki, 0)
    in_specs = [pl.BlockSpec((tq, G * 128), q_map)]
    args = [q]
    if rope_ext:
        in_specs.append(pl.BlockSpec((tq, G * 128), q_map))
        args.append(qr)
    in_specs.append(pl.BlockSpec((tk, kvw), kv_map))
    args.append(k)
    if rope_ext:
        in_specs.append(pl.BlockSpec((tk, 128), kr_map))
        args.append(kr)
    in_specs.append(pl.BlockSpec((tk, kvw), kv_map))
    args.append(v)
    scratch = []
    if nk > 1:
        scratch = [pltpu.VMEM((G, tq, 1), F32), pltpu.VMEM((G, tq, 256), F32)]
    return pl.pallas_call(
        functools.partial(_attn_kernel, G=G, heads_per_kv=heads_per_kv, rope_ext=rope_ext, nk=nk),
        out_shape=jax.ShapeDtypeStruct((T, qw), BF16),
        grid=(batch, n_groups, nq, nk),
        in_specs=in_specs,
        out_specs=pl.BlockSpec((tq, G * 128), q_map),
        scratch_shapes=scratch,
        compiler_params=_cparams(("arbitrary",) * 4),
        name="attn_mla" if rope_ext else "attn_gqa",
    )(*args)


def _out_proj_kernel(*refs, widths):
    x_ref, g_ref, w_ref = refs[:3]
    a_refs = refs[3:3 + len(widths)]
    o_ref = refs[3 + len(widths)]
    acc = None
    off = 0
    for a_ref, wd in zip(a_refs, widths):
        part = _dot(a_ref[...], w_ref[off:off + wd, :])
        acc = part if acc is None else acc + part
        off += wd
    o_ref[...] = x_ref[...] + _rms(acc, g_ref[...])


def _out_proj(x, g, w, parts, *, tm):
    T = x.shape[0]
    widths = tuple(wd for _, wd, _ in parts)
    row = lambda i: (i, 0)
    return pl.pallas_call(
        functools.partial(_out_proj_kernel, widths=widths),
        out_shape=jax.ShapeDtypeStruct((T, D_MODEL), F32),
        grid=(T // tm,),
        in_specs=[
            pl.BlockSpec((tm, D_MODEL), row),
            pl.BlockSpec((1, D_MODEL), lambda i: (0, 0)),
            pl.BlockSpec(w.shape, lambda i: (0, 0), pipeline_mode=pl.Buffered(1)),
        ] + [pl.BlockSpec((tm, wd), functools.partial(lambda i, cb: (i, cb), cb=cb)) for _, wd, cb in parts],
        out_specs=pl.BlockSpec((tm, D_MODEL), row),
        compiler_params=_cparams(("arbitrary",)),
        name="out_proj",
    )(x, g, w, *[a for a, _, _ in parts])


def _mlp_kernel(x_ref, gpre_ref, gpost_ref, wup_ref, wdn_ref, o_ref, hn_sc, *, nf):
    f = pl.program_id(1)

    @pl.when(f == 0)
    def _():
        hn_sc[...] = _rms(x_ref[...], gpre_ref[...]).astype(BF16)
        o_ref[...] = jnp.zeros(o_ref.shape, F32)

    h = jnp.maximum(_dot(hn_sc[...], wup_ref[...]), 0.0)
    o_ref[...] += _dot((h * h).astype(BF16), wdn_ref[...])

    @pl.when(f == nf - 1)
    def _():
        o_ref[...] = x_ref[...] + _rms(o_ref[...], gpost_ref[...])


def _mlp(x, gpre, gpost, wup, wdn, layer, *, tm, tf):
    T = x.shape[0]
    nf = D_FF // tf
    row = lambda i, f: (i, 0)
    fixed = lambda i, f: (0, 0)
    return pl.pallas_call(
        functools.partial(_mlp_kernel, nf=nf),
        out_shape=jax.ShapeDtypeStruct((T, D_MODEL), F32),
        grid=(T // tm, nf),
        in_specs=[
            pl.BlockSpec((tm, D_MODEL), row),
            pl.BlockSpec((1, D_MODEL), fixed),
            pl.BlockSpec((1, D_MODEL), fixed),
            pl.BlockSpec((None, D_MODEL, tf), lambda i, f: (layer, 0, f)),
            pl.BlockSpec((None, tf, D_MODEL), lambda i, f: (layer, f, 0)),
        ],
        out_specs=pl.BlockSpec((tm, D_MODEL), row),
        scratch_shapes=[pltpu.VMEM((tm, D_MODEL), BF16)],
        compiler_params=_cparams(("arbitrary", "arbitrary")),
        name="mlp",
    )(x, gpre, gpost, wup, wdn)


ODD_TN = 1024


def _odd_proj_kernel(x_ref, gpre_ref, w_ref, *rest, rope, n_q_blocks):
    if rope:
        c_ref, s_ref, o_ref, hn_sc = rest
    else:
        o_ref, hn_sc = rest
    j = pl.program_id(1)

    @pl.when(j == 0)
    def _():
        hn_sc[...] = _rms(x_ref[...], gpre_ref[...]).astype(BF16)

    acc = _dot(hn_sc[...], w_ref[...])
    if not rope:
        o_ref[...] = acc.astype(BF16)
        return
    scale = jnp.where(j < n_q_blocks, 1.0, C_QK_DIM ** -0.5).astype(F32)
    c = c_ref[...] * scale
    s = s_ref[...] * scale
    for h in range(ODD_TN // C_QK_DIM):
        x0 = acc[:, h * 256:h * 256 + 128]
        x1 = acc[:, h * 256 + 128:(h + 1) * 256]
        o_ref[:, h * 256:h * 256 + 128] = (x0 * c - x1 * s).astype(BF16)
        o_ref[:, h * 256 + 128:(h + 1) * 256] = (x0 * s + x1 * c).astype(BF16)


def _odd_proj(x, gpre, w, *, tm, seq=None, tables=None):
    T = x.shape[0]
    N = w.shape[1]
    rope = tables is not None
    in_specs = [
        pl.BlockSpec((tm, D_MODEL), lambda i, j: (i, 0)),
        pl.BlockSpec((1, D_MODEL), lambda i, j: (0, 0)),
        pl.BlockSpec((D_MODEL, ODD_TN), lambda i, j: (0, j)),
    ]
    args = [x, gpre, w]
    if rope:
        nblk_seq = seq // tm
        in_specs += [pl.BlockSpec((tm, 128), lambda i, j: (i % nblk_seq, 0))] * 2
        args += list(tables)
    return pl.pallas_call(
        functools.partial(_odd_proj_kernel, rope=rope, n_q_blocks=N // ODD_TN // 2),
        out_shape=jax.ShapeDtypeStruct((T, N), BF16),
        grid=(T // tm, N // ODD_TN),
        in_specs=in_specs,
        out_specs=pl.BlockSpec((tm, ODD_TN), lambda i, j: (i, j)),
        scratch_shapes=[pltpu.VMEM((tm, D_MODEL), BF16)],
        compiler_params=_cparams(("arbitrary", "arbitrary")),
        name="odd_qk_proj" if rope else "odd_vg_proj",
    )(*args)


def _ret_fwd_kernel(lgf_ref, lgb_ref, q_ref, k_ref, v_ref, y_ref, st_sc, d_sc, qd_sc, kd_sc, cd_sc, *, C, n_chunks):
    h = pl.program_id(1)
    lgf = lgf_ref[h]
    lgb = lgb_ref[h]

    @pl.when(pl.program_id(2) == 0)
    def _():
        st_sc[...] = jnp.zeros(st_sc.shape, F32)
        cd_sc[...] = jnp.exp(jnp.full(cd_sc.shape, C, F32) * lgf)
        i = lax.broadcasted_iota(jnp.int32, (C, C), 0)
        j = lax.broadcasted_iota(jnp.int32, (C, C), 1)
        rel = (i - j).astype(F32)
        d_sc[...] = jnp.where(rel >= 0, jnp.exp(rel * lgf), jnp.exp(-rel * lgb))
        idx = lax.broadcasted_iota(jnp.int32, (C, C_QK_DIM), 0).astype(F32)
        qd_sc[...] = jnp.exp((idx + 1.0) * lgf)
        kd_sc[...] = jnp.exp((C - 1.0 - idx) * lgf)

    st = st_sc[...]
    for c in range(n_chunks):
        rows = slice(c * C, (c + 1) * C)
        q, k, v = q_ref[rows, :], k_ref[rows, :], v_ref[rows, :]
        a = (_dot_nt(q, k) * d_sc[...]).astype(BF16)
        qd = (q.astype(F32) * qd_sc[...]).astype(BF16)
        y_ref[rows, :] = (_dot(a, v) + _dot(qd, st.astype(BF16))).astype(y_ref.dtype)
        kd = (k.astype(F32) * kd_sc[...]).astype(BF16)
        st = st * cd_sc[...] + _dot_tn(kd, v)
    st_sc[...] = st


def _ret_bwd_kernel(lgb_ref, q_ref, k_ref, v_ref, g_ref, yp_ref, o_ref, st_sc, qd_sc, kd_sc, cd_sc, *, C, n_chunks):
    h = pl.program_id(1)
    lgb = lgb_ref[h]

    @pl.when(pl.program_id(2) == 0)
    def _():
        st_sc[...] = jnp.zeros(st_sc.shape, F32)
        cd_sc[...] = jnp.exp(jnp.full(cd_sc.shape, C, F32) * lgb)
        idx = lax.broadcasted_iota(jnp.int32, (C, C_QK_DIM), 0).astype(F32)
        qd_sc[...] = jnp.exp((C - idx) * lgb)
        kd_sc[...] = jnp.exp(idx * lgb)

    st = st_sc[...]
    for c in reversed(range(n_chunks)):
        rows = slice(c * C, (c + 1) * C)
        q, k, v = q_ref[rows, :], k_ref[rows, :], v_ref[rows, :]
        qd = (q.astype(F32) * qd_sc[...]).astype(BF16)
        y = yp_ref[rows, :].astype(F32) + _dot(qd, st.astype(BF16))
        kd = (k.astype(F32) * kd_sc[...]).astype(BF16)
        st = st * cd_sc[...] + _dot_tn(kd, v)
        mu = jnp.mean(y, axis=-1, keepdims=True)
        yc = y - mu
        var = jnp.mean(yc * yc, axis=-1, keepdims=True)
        yn = yc * lax.rsqrt(var + GN_EPS)
        g = g_ref[rows, :].astype(F32)
        o_ref[rows, :] = (g * jax.nn.sigmoid(g) * yn).astype(BF16)
    st_sc[...] = st


def _retention(qk, vg, lgf, lgb, *, batch, seq, C, rows):
    T = qk.shape[0]
    nb = seq // rows
    n_chunks = rows // C
    H = C_HEADS
    fwd = lambda b, h, r: b * nb + r
    bwd = lambda b, h, r: b * nb + (nb - 1 - r)
    qk_blk = (rows, C_QK_DIM)
    v_blk = (rows, C_V_DIM)
    state_scratch = [pltpu.VMEM((C_QK_DIM, C_V_DIM), F32)]
    dec_scratch = [pltpu.VMEM((C, C_QK_DIM), F32), pltpu.VMEM((C, C_QK_DIM), F32), pltpu.VMEM((1, C_V_DIM), F32)]
    y_part = pl.pallas_call(
        functools.partial(_ret_fwd_kernel, C=C, n_chunks=n_chunks),
        out_shape=jax.ShapeDtypeStruct((T, H * C_V_DIM), BF16),
        grid_spec=pltpu.PrefetchScalarGridSpec(
            num_scalar_prefetch=2,
            grid=(batch, H, nb),
            in_specs=[
                pl.BlockSpec(qk_blk, lambda b, h, r, *_: (fwd(b, h, r), h)),
                pl.BlockSpec(qk_blk, lambda b, h, r, *_: (fwd(b, h, r), H + h)),
                pl.BlockSpec(v_blk, lambda b, h, r, *_: (fwd(b, h, r), h)),
            ],
            out_specs=pl.BlockSpec(v_blk, lambda b, h, r, *_: (fwd(b, h, r), h)),
            scratch_shapes=state_scratch + [pltpu.VMEM((C, C), F32)] + dec_scratch,
        ),
        compiler_params=_cparams(("arbitrary",) * 3),
        name="ret_fwd",
    )(lgf, lgb, qk, qk, vg)
    return pl.pallas_call(
        functools.partial(_ret_bwd_kernel, C=C, n_chunks=n_chunks),
        out_shape=jax.ShapeDtypeStruct((T, H * C_V_DIM), BF16),
        grid_spec=pltpu.PrefetchScalarGridSpec(
            num_scalar_prefetch=1,
            grid=(batch, H, nb),
            in_specs=[
                pl.BlockSpec(qk_blk, lambda b, h, r, *_: (bwd(b, h, r), h)),
                pl.BlockSpec(qk_blk, lambda b, h, r, *_: (bwd(b, h, r), H + h)),
                pl.BlockSpec(v_blk, lambda b, h, r, *_: (bwd(b, h, r), h)),
                pl.BlockSpec(v_blk, lambda b, h, r, *_: (bwd(b, h, r), H + h)),
                pl.BlockSpec(v_blk, lambda b, h, r, *_: (bwd(b, h, r), h)),
            ],
            out_specs=pl.BlockSpec(v_blk, lambda b, h, r, *_: (bwd(b, h, r), h)),
            scratch_shapes=state_scratch + dec_scratch,
        ),
        compiler_params=_cparams(("arbitrary",) * 3),
        name="ret_bwd",
    )(lgb, qk, qk, vg, vg, y_part)


def _deinterleave(d):
    return np.concatenate([np.arange(0, d, 2), np.arange(1, d, 2)])


def _rope_tables(seq_max):
    t = np.arange(seq_max)
    row = (t // GRID_W).astype(np.float32)
    col = (t % GRID_W).astype(np.float32)

    def ang(d_rot):
        n = d_rot // 4
        inv = jnp.asarray(ROPE_THETA, F32) ** (-jnp.arange(n, dtype=F32) / n)
        return jnp.concatenate([jnp.asarray(row)[:, None] * inv, jnp.asarray(col)[:, None] * inv], axis=-1)

    a = ang(A_HEAD_DIM)
    ca = jnp.concatenate([jnp.cos(a), jnp.cos(a)], axis=-1)
    sa = jnp.concatenate([-jnp.sin(a), jnp.sin(a)], axis=-1)
    b = ang(B_ROPE_DIM)
    z = jnp.zeros_like(b)
    cb = jnp.concatenate([jnp.cos(b), z, jnp.cos(b), z], axis=-1)
    sb = jnp.concatenate([-jnp.sin(b), z, jnp.sin(b), z], axis=-1)
    c = ang(C_QK_DIM)
    return ca, sa, cb, sb, jnp.cos(c), jnp.sin(c)


def _prep_even(w_in, g_qa, g_ka, g_cq, w_uq, g_ckv, w_ukv, w_out):
    perm_a = _deinterleave(A_HEAD_DIM)
    perm_b = _deinterleave(B_ROPE_DIM)
    w_in, w_uq, w_ukv = w_in.astype(BF16), w_uq.astype(BF16), w_ukv.astype(BF16)
    o = 0
    wqa = w_in[:, o:o + 1024].reshape(D_MODEL, A_HEADS, 128)[:, :, perm_a].reshape(D_MODEL, 1024)
    o += 1024
    wka = w_in[:, o:o + 256].reshape(D_MODEL, A_KV_HEADS, 128)[:, :, perm_a].reshape(D_MODEL, 256)
    o += 256
    wva = w_in[:, o:o + 256]
    o += 256
    wcq = w_in[:, o:o + 512]
    o += 512
    wckv = w_in[:, o:o + 256]
    o += 256
    wkr = w_in[:, o:o + 64][:, perm_b]
    z32 = jnp.zeros((D_MODEL, 32), w_in.dtype)
    wkr = jnp.concatenate([wkr[:, :32], z32, wkr[:, 32:], z32], axis=-1)
    w1 = jnp.concatenate([wqa, wka, wva, wcq, wckv, wkr], axis=-1).astype(BF16)
    uq = w_uq.reshape(B_Q_RANK, B_HEADS, B_NOPE_DIM + B_ROPE_DIM)
    uq_nope = uq[:, :, :B_NOPE_DIM].reshape(B_Q_RANK, 1024)
    uq_rope = uq[:, :, B_NOPE_DIM:][:, :, perm_b]
    zr = jnp.zeros((B_Q_RANK, B_HEADS, 32), w_uq.dtype)
    uq_rope = jnp.concatenate([uq_rope[:, :, :32], zr, uq_rope[:, :, 32:], zr], axis=-1).reshape(B_Q_RANK, 1024)
    wuq = jnp.concatenate([uq_nope, uq_rope], axis=-1).astype(BF16)
    ukv = w_ukv.reshape(B_KV_RANK, B_HEADS, B_NOPE_DIM + B_V_DIM)
    wukv = jnp.concatenate([ukv[:, :, :B_NOPE_DIM].reshape(B_KV_RANK, 1024),
                            ukv[:, :, B_NOPE_DIM:].reshape(B_KV_RANK, 1024)], axis=-1).astype(BF16)
    return dict(w1=w1, gqa=g_qa[perm_a][None], gka=g_ka[perm_a][None], gcq=g_cq[None], gckv=g_ckv[None],
                wuq=wuq, wukv=wukv, wout=w_out.astype(BF16))


def _prep_odd(w_in, decay_f, decay_b, w_out):
    perm_c = _deinterleave(C_QK_DIM)
    nqk = C_HEADS * C_QK_DIM
    w_in = w_in.astype(BF16)
    wq = w_in[:, :nqk].reshape(D_MODEL, C_HEADS, C_QK_DIM)[:, :, perm_c].reshape(D_MODEL, nqk)
    wk = w_in[:, nqk:2 * nqk].reshape(D_MODEL, C_HEADS, C_QK_DIM)[:, :, perm_c].reshape(D_MODEL, nqk)
    return dict(wqk=jnp.concatenate([wq, wk], axis=-1).astype(BF16), wvg=w_in[:, 2 * nqk:].astype(BF16),
                lgf=jax.nn.log_sigmoid(decay_f.astype(F32)), lgb=jax.nn.log_sigmoid(decay_b.astype(F32)),
                wout=w_out.astype(BF16))


def _trunk(x, *, batch, seq, tables, norms, even, odd, wup, wdn):
    ca, sa, cb, sb, cc, sc = tables
    g_mix_pre, g_mix_post, g_mlp_pre, g_mlp_post = norms
    for layer in range(DEPTH):
        gpre = g_mix_pre[layer][None]
        gpost = g_mix_post[layer][None]
        if layer % 2 == 0:
            e = even[layer // 2]
            qa, ka, va, qbn, qbr, kbn, vb, kr = _even_proj(
                x, gpre, e["w1"], e["gqa"], e["gka"], e["gcq"], e["gckv"], e["wuq"], e["wukv"],
                ca, sa, cb, sb, seq=seq, tm=512)
            tk, rows_per_step = 2048, 16 * ATTN_ROWS
            oa = _attention(qa, ka, va, batch=batch, seq=seq, G=A_HEADS, heads_per_kv=A_HEADS // A_KV_HEADS,
                            tq=rows_per_step // A_HEADS, tk=tk)
            ob = _attention(qbn, kbn, vb, batch=batch, seq=seq, G=B_HEADS // 2, heads_per_kv=1,
                            tq=rows_per_step // (B_HEADS // 2), tk=tk, qr=qbr, kr=kr)
            x = _out_proj(x, gpost, e["wout"], [(oa, 1024, 0), (ob, 1024, 0)], tm=512)
        else:
            o = odd[layer // 2]
            qk = _odd_proj(x, gpre, o["wqk"], tm=1024, seq=seq, tables=(cc, sc))
            vg = _odd_proj(x, gpre, o["wvg"], tm=1024)
            yg = _retention(qk, vg, o["lgf"], o["lgb"], batch=batch, seq=seq, C=256, rows=2048)
            x = _out_proj(x, gpost, o["wout"], [(yg, 4096, 0)], tm=512)
        x = _mlp(x, g_mlp_pre[layer][None], g_mlp_post[layer][None], wup, wdn, layer, tm=1024, tf=512)
    return x


def kernel(x_prompt, x_sample, g_mix_pre, g_mix_post, g_mlp_pre, g_mlp_post, w_in_even, g_qnorm_a, g_knorm_a,
           g_cq_b, w_uq_b, g_ckv_b, w_ukv_b, w_out_even, w_in_odd, decay_fwd, decay_bwd, w_out_odd,
           w_mlp_up, w_mlp_down):
    n_even, n_odd = w_in_even.shape[0], w_in_odd.shape[0]
    even = [_prep_even(w_in_even[e], g_qnorm_a[e], g_knorm_a[e], g_cq_b[e], w_uq_b[e], g_ckv_b[e], w_ukv_b[e],
                       w_out_even[e]) for e in range(n_even)]
    odd = [_prep_odd(w_in_odd[o], decay_fwd[o], decay_bwd[o], w_out_odd[o]) for o in range(n_odd)]
    wup = w_mlp_up.astype(BF16)
    wdn = w_mlp_down.astype(BF16)
    norms = (g_mix_pre, g_mix_post, g_mlp_pre, g_mlp_post)
    tables = _rope_tables(max(x_prompt.shape[1], x_sample.shape[1]))
    outs = []
    for xg in (x_prompt, x_sample):
        b, s, d = xg.shape
        y = _trunk(xg.reshape(b * s, d), batch=b, seq=s, tables=tables, norms=norms, even=even, odd=odd,
                   wup=wup, wdn=wdn)
        outs.append(y.reshape(b, s, d))
    return tuple(outs)
```

```python
import functools

import jax
import jax.numpy as jnp
import numpy as np
from jax import lax
from jax.experimental import pallas as pl
from jax.experimental.pallas import tpu as pltpu

D_MODEL = 2048
DEPTH = 4
GRID_W = 64
ROPE_THETA = 10000.0
NORM_EPS = 1e-6
GN_EPS = 1e-5
A_HEADS, A_KV_HEADS, A_HEAD_DIM = 8, 2, 128
B_HEADS, B_Q_RANK, B_KV_RANK, B_NOPE_DIM, B_ROPE_DIM, B_V_DIM = 8, 512, 256, 128, 64, 128
C_HEADS, C_QK_DIM, C_V_DIM = 8, 256, 512
D_FF = 4 * D_MODEL

VMEM_LIMIT_BYTES = 56 * 1024 * 1024
ATTN_ROWS = 256
LOG2E = 1.4426950408889634

BF16 = jnp.bfloat16
F32 = jnp.float32


def _cparams(semantics):
    return pltpu.CompilerParams(dimension_semantics=semantics, vmem_limit_bytes=VMEM_LIMIT_BYTES)


def _rms(x, g):
    return x * lax.rsqrt(jnp.mean(x * x, axis=-1, keepdims=True) + NORM_EPS) * g


def _dot(a, b):
    return jnp.dot(a, b, preferred_element_type=F32)


def _dot_nt(a, b):
    return lax.dot_general(a, b, (((1,), (1,)), ((), ())), preferred_element_type=F32)


def _dot_tn(a, b):
    return lax.dot_general(a, b, (((0,), (0,)), ((), ())), preferred_element_type=F32)


def _rope_half(x, c, s):
    return x * c + pltpu.roll(x, 64, 1) * s


EVEN_W1_COLS = 1024 + 256 + 256 + 512 + 256 + 128


def _even_proj_kernel(x_ref, gpre_ref, w1_ref, gqa_ref, gka_ref, gcq_ref, gckv_ref, wuq_ref, wukv_ref,
                      ca_ref, sa_ref, cb_ref, sb_ref,
                      qa_ref, ka_ref, va_ref, qbn_ref, qbr_ref, kbn_ref, vb_ref, kr_ref):
    hn = _rms(x_ref[...], gpre_ref[...]).astype(BF16)
    p = _dot(hn, w1_ref[...])
    ca, sa = ca_ref[...], sa_ref[...]
    cb, sb = cb_ref[...], sb_ref[...]
    gq = gqa_ref[...] * (A_HEAD_DIM ** -0.5 * LOG2E)
    gk = gka_ref[...]
    for h in range(A_HEADS):
        xh = p[:, h * 128:(h + 1) * 128]
        qa_ref[:, h * 128:(h + 1) * 128] = _rope_half(_rms(xh, gq), ca, sa).astype(BF16)
    for h in range(A_KV_HEADS):
        xh = p[:, 1024 + h * 128:1024 + (h + 1) * 128]
        ka_ref[:, h * 128:(h + 1) * 128] = _rope_half(_rms(xh, gk), ca, sa).astype(BF16)
    va_ref[...] = p[:, 1280:1536].astype(BF16)
    cqn = _rms(p[:, 1536:2048], gcq_ref[...]).astype(BF16)
    qb = _dot(cqn, wuq_ref[...]) * ((B_NOPE_DIM + B_ROPE_DIM) ** -0.5 * LOG2E)
    qbn_ref[...] = qb[:, :1024].astype(BF16)
    for h in range(B_HEADS):
        xh = qb[:, 1024 + h * 128:1024 + (h + 1) * 128]
        qbr_ref[:, h * 128:(h + 1) * 128] = _rope_half(xh, cb, sb).astype(BF16)
    ckvn = _rms(p[:, 2048:2304], gckv_ref[...]).astype(BF16)
    kv = _dot(ckvn, wukv_ref[...])
    kbn_ref[...] = kv[:, :1024].astype(BF16)
    vb_ref[...] = kv[:, 1024:].astype(BF16)
    kr_ref[...] = _rope_half(p[:, 2304:2432], cb, sb).astype(BF16)


def _even_proj(x, gpre, w1, gqa, gka, gcq, gckv, wuq, wukv, ca, sa, cb, sb, *, seq, tm):
    T = x.shape[0]
    nblk_seq = seq // tm
    row = lambda i: (i, 0)
    fixed = lambda i: (0, 0)
    pos = lambda i: (i % nblk_seq, 0)
    out_cols = (1024, 256, 256, 1024, 1024, 1024, 1024, 128)
    return pl.pallas_call(
        _even_proj_kernel,
        out_shape=[jax.ShapeDtypeStruct((T, c), BF16) for c in out_cols],
        grid=(T // tm,),
        in_specs=[
            pl.BlockSpec((tm, D_MODEL), row),
            pl.BlockSpec((1, D_MODEL), fixed),
            pl.BlockSpec((D_MODEL, EVEN_W1_COLS), fixed, pipeline_mode=pl.Buffered(1)),
            pl.BlockSpec((1, 128), fixed),
            pl.BlockSpec((1, 128), fixed),
            pl.BlockSpec((1, B_Q_RANK), fixed),
            pl.BlockSpec((1, B_KV_RANK), fixed),
            pl.BlockSpec((B_Q_RANK, 2048), fixed, pipeline_mode=pl.Buffered(1)),
            pl.BlockSpec((B_KV_RANK, 2048), fixed, pipeline_mode=pl.Buffered(1)),
            pl.BlockSpec((tm, 128), pos),
            pl.BlockSpec((tm, 128), pos),
            pl.BlockSpec((tm, 128), pos),
            pl.BlockSpec((tm, 128), pos),
        ],
        out_specs=[pl.BlockSpec((tm, c), row) for c in out_cols],
        compiler_params=_cparams(("arbitrary",)),
        name="even_proj",
    )(x, gpre, w1, gqa, gka, gcq, gckv, wuq, wukv, ca, sa, cb, sb)


def _attn_kernel(*refs, G, heads_per_kv, rope_ext, nk):
    if rope_ext:
        q_ref, qr_ref, k_ref, kr_ref, v_ref, o_ref = refs[:6]
        scratch = refs[6:]
    else:
        q_ref, k_ref, v_ref, o_ref = refs[:4]
        qr_ref = kr_ref = None
        scratch = refs[4:]
    ki = pl.program_id(3)
    tq = q_ref.shape[0]
    items = [(g, slice(r, r + ATTN_ROWS)) for g in range(G) for r in range(0, tq, ATTN_ROWS)]

    def scores(g, rows):
        sl = slice(g * 128, (g + 1) * 128)
        kvh = g // heads_per_kv
        q, k = q_ref[rows, sl], k_ref[:, kvh * 128:(kvh + 1) * 128]
        if rope_ext:
            q = jnp.concatenate([q, qr_ref[rows, sl]], axis=-1)
            k = jnp.concatenate([k, kr_ref[...]], axis=-1)
        return _dot_nt(q, k)

    n_kv = G // heads_per_kv
    ones = jnp.ones((v_ref.shape[0], 128), BF16)

    def values_and_ones(negate):
        ext = [jnp.concatenate([v_ref[:, h * 128:(h + 1) * 128], ones], axis=-1) for h in range(n_kv)]
        return [-e for e in ext] if negate else ext

    def normalised(o_and_sum):
        return (o_and_sum[:, :128] / o_and_sum[:, 128:]).astype(o_ref.dtype)

    if nk == 1:
        v_ext = values_and_ones(negate=False)
        for g, rows in items:
            s = scores(g, rows)
            p = jnp.exp2(s - jnp.max(s, axis=-1, keepdims=True))
            o_ref[rows, g * 128:(g + 1) * 128] = normalised(_dot(p.astype(BF16), v_ext[g // heads_per_kv]))
        return

    m_sc, acc_sc = scratch

    @pl.when(ki == 0)
    def _():
        m_sc[...] = jnp.full(m_sc.shape, -jnp.inf, F32)
        acc_sc[...] = jnp.zeros(acc_sc.shape, F32)

    neg_v_ext = values_and_ones(negate=True)
    for g, rows in items:
        s = scores(g, rows)
        m_prev = m_sc[g, rows]
        m_new = jnp.maximum(m_prev, jnp.max(s, axis=-1, keepdims=True))
        alpha = jnp.exp2(m_prev - m_new)
        p = jnp.exp2(s - m_new)
        acc_sc[g, rows] = alpha * acc_sc[g, rows] - _dot(p.astype(BF16), neg_v_ext[g // heads_per_kv])
        m_sc[g, rows] = m_new

    @pl.when(ki == nk - 1)
    def _():
        for g in range(G):
            o_ref[:, g * 128:(g + 1) * 128] = normalised(acc_sc[g])


def _attention(q, k, v, *, batch, seq, G, heads_per_kv, tq, tk, qr=None, kr=None):
    T, qw = q.shape
    n_groups = qw // (G * 128)
    kvw = G // heads_per_kv * 128
    assert kvw % 256 == 0
    nq, nk = seq // tq, seq // tk
    rope_ext = qr is not None
    q_map = lambda b, h, qi, ki: (b * nq + qi, h)
    kv_map = lambda b, h, qi, ki: (b * nk + ki, h)
    kr_map = lambda b, h, qi, ki: (b * nk + ki, 0)
    in_specs = [pl.BlockSpec((tq, G * 128), q_map)]
    args = [q]
    if rope_ext:
        in_specs.append(pl.BlockSpec((tq, G * 128), q_map))
        args.append(qr)
    in_specs.append(pl.BlockSpec((tk, kvw), kv_map))
    args.append(k)
    if rope_ext:
        in_specs.append(pl.BlockSpec((tk, 128), kr_map))
        args.append(kr)
    in_specs.append(pl.BlockSpec((tk, kvw), kv_map))
    args.append(v)
    scratch = []
    if nk > 1:
        scratch = [pltpu.VMEM((G, tq, 1), F32), pltpu.VMEM((G, tq, 256), F32)]
    return pl.pallas_call(
        functools.partial(_attn_kernel, G=G, heads_per_kv=heads_per_kv, rope_ext=rope_ext, nk=nk),
        out_shape=jax.ShapeDtypeStruct((T, qw), BF16),
        grid=(batch, n_groups, nq, nk),
        in_specs=in_specs,
        out_specs=pl.BlockSpec((tq, G * 128), q_map),
        scratch_shapes=scratch,
        compiler_params=_cparams(("arbitrary",) * 4),
        name="attn_mla" if rope_ext else "attn_gqa",
    )(*args)


def _out_proj_kernel(*refs, widths):
    x_ref, g_ref, w_ref = refs[:3]
    a_refs = refs[3:3 + len(widths)]
    o_ref = refs[3 + len(widths)]
    acc = None
    off = 0
    for a_ref, wd in zip(a_refs, widths):
        part = _dot(a_ref[...], w_ref[off:off + wd, :])
        acc = part if acc is None else acc + part
        off += wd
    o_ref[...] = x_ref[...] + _rms(acc, g_ref[...])


def _out_proj(x, g, w, parts, *, tm):
    T = x.shape[0]
    widths = tuple(wd for _, wd, _ in parts)
    row = lambda i: (i, 0)
    return pl.pallas_call(
        functools.partial(_out_proj_kernel, widths=widths),
        out_shape=jax.ShapeDtypeStruct((T, D_MODEL), F32),
        grid=(T // tm,),
        in_specs=[
            pl.BlockSpec((tm, D_MODEL), row),
            pl.BlockSpec((1, D_MODEL), lambda i: (0, 0)),
            pl.BlockSpec(w.shape, lambda i: (0, 0), pipeline_mode=pl.Buffered(1)),
        ] + [pl.BlockSpec((tm, wd), functools.partial(lambda i, cb: (i, cb), cb=cb)) for _, wd, cb in parts],
        out_specs=pl.BlockSpec((tm, D_MODEL), row),
        compiler_params=_cparams(("arbitrary",)),
        name="out_proj",
    )(x, g, w, *[a for a, _, _ in parts])


def _mlp_kernel(x_ref, gpre_ref, gpost_ref, wup_ref, wdn_ref, o_ref, hn_sc, *, nf):
    f = pl.program_id(1)

    @pl.when(f == 0)
    def _():
        hn_sc[...] = _rms(x_ref[...], gpre_ref[...]).astype(BF16)
        o_ref[...] = jnp.zeros(o_ref.shape, F32)

    h = jnp.maximum(_dot(hn_sc[...], wup_ref[...]), 0.0)
    o_ref[...] += _dot((h * h).astype(BF16), wdn_ref[...])

    @pl.when(f == nf - 1)
    def _():
        o_ref[...] = x_ref[...] + _rms(o_ref[...], gpost_ref[...])


def _mlp(x, gpre, gpost, wup, wdn, layer, *, tm, tf):
    T = x.shape[0]
    nf = D_FF // tf
    row = lambda i, f: (i, 0)
    fixed = lambda i, f: (0, 0)
    return pl.pallas_call(
        functools.partial(_mlp_kernel, nf=nf),
        out_shape=jax.ShapeDtypeStruct((T, D_MODEL), F32),
        grid=(T // tm, nf),
        in_specs=[
            pl.BlockSpec((tm, D_MODEL), row),
            pl.BlockSpec((1, D_MODEL), fixed),
            pl.BlockSpec((1, D_MODEL), fixed),
            pl.BlockSpec((None, D_MODEL, tf), lambda i, f: (layer, 0, f)),
            pl.BlockSpec((None, tf, D_MODEL), lambda i, f: (layer, f, 0)),
        ],
        out_specs=pl.BlockSpec((tm, D_MODEL), row),
        scratch_shapes=[pltpu.VMEM((tm, D_MODEL), BF16)],
        compiler_params=_cparams(("arbitrary", "arbitrary")),
        name="mlp",
    )(x, gpre, gpost, wup, wdn)


ODD_TN = 1024


def _odd_proj_kernel(x_ref, gpre_ref, w_ref, *rest, rope, n_q_blocks):
    if rope:
        c_ref, s_ref, o_ref, hn_sc = rest
    else:
        o_ref, hn_sc = rest
    j = pl.program_id(1)

    @pl.when(j == 0)
    def _():
        hn_sc[...] = _rms(x_ref[...], gpre_ref[...]).astype(BF16)

    acc = _dot(hn_sc[...], w_ref[...])
    if not rope:
        o_ref[...] = acc.astype(BF16)
        return
    scale = jnp.where(j < n_q_blocks, 1.0, C_QK_DIM ** -0.5).astype(F32)
    c = c_ref[...] * scale
    s = s_ref[...] * scale
    for h in range(ODD_TN // C_QK_DIM):
        x0 = acc[:, h * 256:h * 256 + 128]
        x1 = acc[:, h * 256 + 128:(h + 1) * 256]
        o_ref[:, h * 256:h * 256 + 128] = (x0 * c - x1 * s).astype(BF16)
        o_ref[:, h * 256 + 128:(h + 1) * 256] = (x0 * s + x1 * c).astype(BF16)


def _odd_proj(x, gpre, w, *, tm, seq=None, tables=None):
    T = x.shape[0]
    N = w.shape[1]
    rope = tables is not None
    in_specs = [
        pl.BlockSpec((tm, D_MODEL), lambda i, j: (i, 0)),
        pl.BlockSpec((1, D_MODEL), lambda i, j: (0, 0)),
        pl.BlockSpec((D_MODEL, ODD_TN), lambda i, j: (0, j)),
    ]
    args = [x, gpre, w]
    if rope:
        nblk_seq = seq // tm
        in_specs += [pl.BlockSpec((tm, 128), lambda i, j: (i % nblk_seq, 0))] * 2
        args += list(tables)
    return pl.pallas_call(
        functools.partial(_odd_proj_kernel, rope=rope, n_q_blocks=N // ODD_TN // 2),
        out_shape=jax.ShapeDtypeStruct((T, N), BF16),
        grid=(T // tm, N // ODD_TN),
        in_specs=in_specs,
        out_specs=pl.BlockSpec((tm, ODD_TN), lambda i, j: (i, j)),
        scratch_shapes=[pltpu.VMEM((tm, D_MODEL), BF16)],
        compiler_params=_cparams(("arbitrary", "arbitrary")),
        name="odd_qk_proj" if rope else "odd_vg_proj",
    )(*args)


def _ret_fwd_kernel(lgf_ref, lgb_ref, q_ref, k_ref, v_ref, y_ref, st_sc, d_sc, qd_sc, kd_sc, cd_sc, *, C, n_chunks):
    h = pl.program_id(1)
    lgf = lgf_ref[h]
    lgb = lgb_ref[h]

    @pl.when(pl.program_id(2) == 0)
    def _():
        st_sc[...] = jnp.zeros(st_sc.shape, F32)
        cd_sc[...] = jnp.exp(jnp.full(cd_sc.shape, C, F32) * lgf)
        i = lax.broadcasted_iota(jnp.int32, (C, C), 0)
        j = lax.broadcasted_iota(jnp.int32, (C, C), 1)
        rel = (i - j).astype(F32)
        d_sc[...] = jnp.where(rel >= 0, jnp.exp(rel * lgf), jnp.exp(-rel * lgb))
        idx = lax.broadcasted_iota(jnp.int32, (C, C_QK_DIM), 0).astype(F32)
        qd_sc[...] = jnp.exp((idx + 1.0) * lgf)
        kd_sc[...] = jnp.exp((C - 1.0 - idx) * lgf)

    st = st_sc[...]
    for c in range(n_chunks):
        rows = slice(c * C, (c + 1) * C)
        q, k, v = q_ref[rows, :], k_ref[rows, :], v_ref[rows, :]
        a = (_dot_nt(q, k) * d_sc[...]).astype(BF16)
        qd = (q.astype(F32) * qd_sc[...]).astype(BF16)
        y_ref[rows, :] = (_dot(a, v) + _dot(qd, st.astype(BF16))).astype(y_ref.dtype)
        kd = (k.astype(F32) * kd_sc[...]).astype(BF16)
        st = st * cd_sc[...] + _dot_tn(kd, v)
    st_sc[...] = st


def _ret_bwd_kernel(lgb_ref, q_ref, k_ref, v_ref, g_ref, yp_ref, o_ref, st_sc, qd_sc, kd_sc, cd_sc, *, C, n_chunks):
    h = pl.program_id(1)
    lgb = lgb_ref[h]

    @pl.when(pl.program_id(2) == 0)
    def _():
        st_sc[...] = jnp.zeros(st_sc.shape, F32)
        cd_sc[...] = jnp.exp(jnp.full(cd_sc.shape, C, F32) * lgb)
        idx = lax.broadcasted_iota(jnp.int32, (C, C_QK_DIM), 0).astype(F32)
        qd_sc[...] = jnp.exp((C - idx) * lgb)
        kd_sc[...] = jnp.exp(idx * lgb)

    st = st_sc[...]
    for c in reversed(range(n_chunks)):
        rows = slice(c * C, (c + 1) * C)
        q, k, v = q_ref[rows, :], k_ref[rows, :], v_ref[rows, :]
        qd = (q.astype(F32) * qd_sc[...]).astype(BF16)
        y = yp_ref[rows, :].astype(F32) + _dot(qd, st.astype(BF16))
        kd = (k.astype(F32) * kd_sc[...]).astype(BF16)
        st = st * cd_sc[...] + _dot_tn(kd, v)
        mu = jnp.mean(y, axis=-1, keepdims=True)
        yc = y - mu
        var = jnp.mean(yc * yc, axis=-1, keepdims=True)
        yn = yc * lax.rsqrt(var + GN_EPS)
        g = g_ref[rows, :].astype(F32)
        o_ref[rows, :] = (g * jax.nn.sigmoid(g) * yn).astype(BF16)
    st_sc[...] = st


def _retention(qk, vg, lgf, lgb, *, batch, seq, C, rows):
    T = qk.shape[0]
    nb = seq // rows
    n_chunks = rows // C
    H = C_HEADS
    fwd = lambda b, h, r: b * nb + r
    bwd = lambda b, h, r: b * nb + (nb - 1 - r)
    qk_blk = (rows, C_QK_DIM)
    v_blk = (rows, C_V_DIM)
    state_scratch = [pltpu.VMEM((C_QK_DIM, C_V_DIM), F32)]
    dec_scratch = [pltpu.VMEM((C, C_QK_DIM), F32), pltpu.VMEM((C, C_QK_DIM), F32), pltpu.VMEM((1, C_V_DIM), F32)]
    y_part = pl.pallas_call(
        functools.partial(_ret_fwd_kernel, C=C, n_chunks=n_chunks),
        out_shape=jax.ShapeDtypeStruct((T, H * C_V_DIM), BF16),
        grid_spec=pltpu.PrefetchScalarGridSpec(
            num_scalar_prefetch=2,
            grid=(batch, H, nb),
            in_specs=[
                pl.BlockSpec(qk_blk, lambda b, h, r, *_: (fwd(b, h, r), h)),
                pl.BlockSpec(qk_blk, lambda b, h, r, *_: (fwd(b, h, r), H + h)),
                pl.BlockSpec(v_blk, lambda b, h, r, *_: (fwd(b, h, r), h)),
            ],
            out_specs=pl.BlockSpec(v_blk, lambda b, h, r, *_: (fwd(b, h, r), h)),
            scratch_shapes=state_scratch + [pltpu.VMEM((C, C), F32)] + dec_scratch,
        ),
        compiler_params=_cparams(("arbitrary",) * 3),
        name="ret_fwd",
    )(lgf, lgb, qk, qk, vg)
    return pl.pallas_call(
        functools.partial(_ret_bwd_kernel, C=C, n_chunks=n_chunks),
        out_shape=jax.ShapeDtypeStruct((T, H * C_V_DIM), BF16),
        grid_spec=pltpu.PrefetchScalarGridSpec(
            num_scalar_prefetch=1,
            grid=(batch, H, nb),
            in_specs=[
                pl.BlockSpec(qk_blk, lambda b, h, r, *_: (bwd(b, h, r), h)),
                pl.BlockSpec(qk_blk, lambda b, h, r, *_: (bwd(b, h, r), H + h)),
                pl.BlockSpec(v_blk, lambda b, h, r, *_: (bwd(b, h, r), h)),
                pl.BlockSpec(v_blk, lambda b, h, r, *_: (bwd(b, h, r), H + h)),
                pl.BlockSpec(v_blk, lambda b, h, r, *_: (bwd(b, h, r), h)),
            ],
            out_specs=pl.BlockSpec(v_blk, lambda b, h, r, *_: (bwd(b, h, r), h)),
            scratch_shapes=state_scratch + dec_scratch,
        ),
        compiler_params=_cparams(("arbitrary",) * 3),
        name="ret_bwd",
    )(lgb, qk, qk, vg, vg, y_part)


def _deinterleave(d):
    return np.concatenate([np.arange(0, d, 2), np.arange(1, d, 2)])


def _rope_tables(seq_max):
    t = np.arange(seq_max)
    row = (t // GRID_W).astype(np.float32)
    col = (t % GRID_W).astype(np.float32)

    def ang(d_rot):
        n = d_rot // 4
        inv = jnp.asarray(ROPE_THETA, F32) ** (-jnp.arange(n, dtype=F32) / n)
        return jnp.concatenate([jnp.asarray(row)[:, None] * inv, jnp.asarray(col)[:, None] * inv], axis=-1)

    a = ang(A_HEAD_DIM)
    ca = jnp.concatenate([jnp.cos(a), jnp.cos(a)], axis=-1)
    sa = jnp.concatenate([-jnp.sin(a), jnp.sin(a)], axis=-1)
    b = ang(B_ROPE_DIM)
    z = jnp.zeros_like(b)
    cb = jnp.concatenate([jnp.cos(b), z, jnp.cos(b), z], axis=-1)
    sb = jnp.concatenate([-jnp.sin(b), z, jnp.sin(b), z], axis=-1)
    c = ang(C_QK_DIM)
    return ca, sa, cb, sb, jnp.cos(c), jnp.sin(c)


def _prep_even(w_in, g_qa, g_ka, g_cq, w_uq, g_ckv, w_ukv, w_out):
    perm_a = _deinterleave(A_HEAD_DIM)
    perm_b = _deinterleave(B_ROPE_DIM)
    w_in, w_uq, w_ukv = w_in.astype(BF16), w_uq.astype(BF16), w_ukv.astype(BF16)
    o = 0
    wqa = w_in[:, o:o + 1024].reshape(D_MODEL, A_HEADS, 128)[:, :, perm_a].reshape(D_MODEL, 1024)
    o += 1024
    wka = w_in[:, o:o + 256].reshape(D_MODEL, A_KV_HEADS, 128)[:, :, perm_a].reshape(D_MODEL, 256)
    o += 256
    wva = w_in[:, o:o + 256]
    o += 256
    wcq = w_in[:, o:o + 512]
    o += 512
    wckv = w_in[:, o:o + 256]
    o += 256
    wkr = w_in[:, o:o + 64][:, perm_b]
    z32 = jnp.zeros((D_MODEL, 32), w_in.dtype)
    wkr = jnp.concatenate([wkr[:, :32], z32, wkr[:, 32:], z32], axis=-1)
    w1 = jnp.concatenate([wqa, wka, wva, wcq, wckv, wkr], axis=-1).astype(BF16)
    uq = w_uq.reshape(B_Q_RANK, B_HEADS, B_NOPE_DIM + B_ROPE_DIM)
    uq_nope = uq[:, :, :B_NOPE_DIM].reshape(B_Q_RANK, 1024)
    uq_rope = uq[:, :, B_NOPE_DIM:][:, :, perm_b]
    zr = jnp.zeros((B_Q_RANK, B_HEADS, 32), w_uq.dtype)
    uq_rope = jnp.concatenate([uq_rope[:, :, :32], zr, uq_rope[:, :, 32:], zr], axis=-1).reshape(B_Q_RANK, 1024)
    wuq = jnp.concatenate([uq_nope, uq_rope], axis=-1).astype(BF16)
    ukv = w_ukv.reshape(B_KV_RANK, B_HEADS, B_NOPE_DIM + B_V_DIM)
    wukv = jnp.concatenate([ukv[:, :, :B_NOPE_DIM].reshape(B_KV_RANK, 1024),
                            ukv[:, :, B_NOPE_DIM:].reshape(B_KV_RANK, 1024)], axis=-1).astype(BF16)
    return dict(w1=w1, gqa=g_qa[perm_a][None], gka=g_ka[perm_a][None], gcq=g_cq[None], gckv=g_ckv[None],
                wuq=wuq, wukv=wukv, wout=w_out.astype(BF16))


def _prep_odd(w_in, decay_f, decay_b, w_out):
    perm_c = _deinterleave(C_QK_DIM)
    nqk = C_HEADS * C_QK_DIM
    w_in = w_in.astype(BF16)
    wq = w_in[:, :nqk].reshape(D_MODEL, C_HEADS, C_QK_DIM)[:, :, perm_c].reshape(D_MODEL, nqk)
    wk = w_in[:, nqk:2 * nqk].reshape(D_MODEL, C_HEADS, C_QK_DIM)[:, :, perm_c].reshape(D_MODEL, nqk)
    return dict(wqk=jnp.concatenate([wq, wk], axis=-1).astype(BF16), wvg=w_in[:, 2 * nqk:].astype(BF16),
                lgf=jax.nn.log_sigmoid(decay_f.astype(F32)), lgb=jax.nn.log_sigmoid(decay_b.astype(F32)),
                wout=w_out.astype(BF16))


def _trunk(x, *, batch, seq, tables, norms, even, odd, wup, wdn):
    ca, sa, cb, sb, cc, sc = tables
    g_mix_pre, g_mix_post, g_mlp_pre, g_mlp_post = norms
    for layer in range(DEPTH):
        gpre = g_mix_pre[layer][None]
        gpost = g_mix_post[layer][None]
        if layer % 2 == 0:
            e = even[layer // 2]
            qa, ka, va, qbn, qbr, kbn, vb, kr = _even_proj(
                x, gpre, e["w1"], e["gqa"], e["gka"], e["gcq"], e["gckv"], e["wuq"], e["wukv"],
                ca, sa, cb, sb, seq=seq, tm=512)
            tk, rows_per_step = 2048, 32 * ATTN_ROWS
            oa = _attention(qa, ka, va, batch=batch, seq=seq, G=A_HEADS, heads_per_kv=A_HEADS // A_KV_HEADS,
                            tq=rows_per_step // A_HEADS, tk=tk)
            ob = _attention(qbn, kbn, vb, batch=batch, seq=seq, G=B_HEADS // 2, heads_per_kv=1,
                            tq=rows_per_step // (B_HEADS // 2), tk=tk, qr=qbr, kr=kr)
            x = _out_proj(x, gpost, e["wout"], [(oa, 1024, 0), (ob, 1024, 0)], tm=512)
        else:
            o = odd[layer // 2]
            qk = _odd_proj(x, gpre, o["wqk"], tm=1024, seq=seq, tables=(cc, sc))
            vg = _odd_proj(x, gpre, o["wvg"], tm=1024)
            yg = _retention(qk, vg, o["lgf"], o["lgb"], batch=batch, seq=seq, C=256, rows=2048)
            x = _out_proj(x, gpost, o["wout"], [(yg, 4096, 0)], tm=512)
        x = _mlp(x, g_mlp_pre[layer][None], g_mlp_post[layer][None], wup, wdn, layer, tm=1024, tf=512)
    return x


def kernel(x_prompt, x_sample, g_mix_pre, g_mix_post, g_mlp_pre, g_mlp_post, w_in_even, g_qnorm_a, g_knorm_a,
           g_cq_b, w_uq_b, g_ckv_b, w_ukv_b, w_out_even, w_in_odd, decay_fwd, decay_bwd, w_out_odd,
           w_mlp_up, w_mlp_down):
    n_even, n_odd = w_in_even.shape[0], w_in_odd.shape[0]
    even = [_prep_even(w_in_even[e], g_qnorm_a[e], g_knorm_a[e], g_cq_b[e], w_uq_b[e], g_ckv_b[e], w_ukv_b[e],
                       w_out_even[e]) for e in range(n_even)]
    odd = [_prep_odd(w_in_odd[o], decay_fwd[o], decay_bwd[o], w_out_odd[o]) for o in range(n_odd)]
    wup = w_mlp_up.astype(BF16)
    wdn = w_mlp_down.astype(BF16)
    norms = (g_mix_pre, g_mix_post, g_mlp_pre, g_mlp_post)
    tables = _rope_tables(max(x_prompt.shape[1], x_sample.shape[1]))
    outs = []
    for xg in (x_prompt, x_sample):
        b, s, d = xg.shape
        y = _trunk(xg.reshape(b * s, d), batch=b, seq=s, tables=tables, norms=norms, even=even, odd=odd,
                   wup=wup, wdn=wdn)
        outs.append(y.reshape(b, s, d))
    return tuple(outs)
```

```python
import functools

import jax
import jax.numpy as jnp
import numpy as np
from jax import lax
from jax.experimental import pallas as pl
from jax.experimental.pallas import tpu as pltpu

D_MODEL = 2048
DEPTH = 4
GRID_W = 64
ROPE_THETA = 10000.0
NORM_EPS = 1e-6
GN_EPS = 1e-5
A_HEADS, A_KV_HEADS, A_HEAD_DIM = 8, 2, 128
B_HEADS, B_Q_RANK, B_KV_RANK, B_NOPE_DIM, B_ROPE_DIM, B_V_DIM = 8, 512, 256, 128, 64, 128
C_HEADS, C_QK_DIM, C_V_DIM = 8, 256, 512
D_FF = 4 * D_MODEL

VMEM_LIMIT_BYTES = 56 * 1024 * 1024
ATTN_ROWS = 256
LOG2E = 1.4426950408889634
NORM_ROWS = 128

BF16 = jnp.bfloat16
F32 = jnp.float32


def _cparams(semantics):
    return pltpu.CompilerParams(dimension_semantics=semantics, vmem_limit_bytes=VMEM_LIMIT_BYTES)


def _rms(x, g):
    return x * lax.rsqrt(jnp.mean(x * x, axis=-1, keepdims=True) + NORM_EPS) * g


def _dot(a, b):
    return jnp.dot(a, b, preferred_element_type=F32)


def _dot_nt(a, b):
    return lax.dot_general(a, b, (((1,), (1,)), ((), ())), preferred_element_type=F32)


def _dot_tn(a, b):
    return lax.dot_general(a, b, (((0,), (0,)), ((), ())), preferred_element_type=F32)


def _rope_half(x, c, s):
    return x * c + pltpu.roll(x, 64, 1) * s


EVEN_W1_COLS = 1024 + 256 + 256 + 512 + 256 + 128


def _even_proj_kernel(x_ref, gpre_ref, w1_ref, gqa_ref, gka_ref, gcq_ref, gckv_ref, wuq_ref, wukv_ref,
                      ca_ref, sa_ref, cb_ref, sb_ref,
                      qa_ref, ka_ref, va_ref, qbn_ref, qbr_ref, kbn_ref, vb_ref, kr_ref):
    hn = _rms(x_ref[...], gpre_ref[...]).astype(BF16)
    p = _dot(hn, w1_ref[...])
    ca, sa = ca_ref[...], sa_ref[...]
    cb, sb = cb_ref[...], sb_ref[...]
    gq = gqa_ref[...] * (A_HEAD_DIM ** -0.5 * LOG2E)
    gk = gka_ref[...]
    for h in range(A_HEADS):
        xh = p[:, h * 128:(h + 1) * 128]
        qa_ref[:, h * 128:(h + 1) * 128] = _rope_half(_rms(xh, gq), ca, sa).astype(BF16)
    for h in range(A_KV_HEADS):
        xh = p[:, 1024 + h * 128:1024 + (h + 1) * 128]
        ka_ref[:, h * 128:(h + 1) * 128] = _rope_half(_rms(xh, gk), ca, sa).astype(BF16)
    va_ref[...] = p[:, 1280:1536].astype(BF16)
    cqn = _rms(p[:, 1536:2048], gcq_ref[...]).astype(BF16)
    qb = _dot(cqn, wuq_ref[...]) * ((B_NOPE_DIM + B_ROPE_DIM) ** -0.5 * LOG2E)
    qbn_ref[...] = qb[:, :1024].astype(BF16)
    for h in range(B_HEADS):
        xh = qb[:, 1024 + h * 128:1024 + (h + 1) * 128]
        qbr_ref[:, h * 128:(h + 1) * 128] = _rope_half(xh, cb, sb).astype(BF16)
    ckvn = _rms(p[:, 2048:2304], gckv_ref[...]).astype(BF16)
    kv = _dot(ckvn, wukv_ref[...])
    kbn_ref[...] = kv[:, :1024].astype(BF16)
    vb_ref[...] = kv[:, 1024:].astype(BF16)
    kr_ref[...] = _rope_half(p[:, 2304:2432], cb, sb).astype(BF16)


def _even_proj(x, gpre, w1, gqa, gka, gcq, gckv, wuq, wukv, ca, sa, cb, sb, *, seq, tm):
    T = x.shape[0]
    nblk_seq = seq // tm
    row = lambda i: (i, 0)
    fixed = lambda i: (0, 0)
    pos = lambda i: (i % nblk_seq, 0)
    out_cols = (1024, 256, 256, 1024, 1024, 1024, 1024, 128)
    return pl.pallas_call(
        _even_proj_kernel,
        out_shape=[jax.ShapeDtypeStruct((T, c), BF16) for c in out_cols],
        grid=(T // tm,),
        in_specs=[
            pl.BlockSpec((tm, D_MODEL), row),
            pl.BlockSpec((1, D_MODEL), fixed),
            pl.BlockSpec((D_MODEL, EVEN_W1_COLS), fixed, pipeline_mode=pl.Buffered(1)),
            pl.BlockSpec((1, 128), fixed),
            pl.BlockSpec((1, 128), fixed),
            pl.BlockSpec((1, B_Q_RANK), fixed),
            pl.BlockSpec((1, B_KV_RANK), fixed),
            pl.BlockSpec((B_Q_RANK, 2048), fixed, pipeline_mode=pl.Buffered(1)),
            pl.BlockSpec((B_KV_RANK, 2048), fixed, pipeline_mode=pl.Buffered(1)),
            pl.BlockSpec((tm, 128), pos),
            pl.BlockSpec((tm, 128), pos),
            pl.BlockSpec((tm, 128), pos),
            pl.BlockSpec((tm, 128), pos),
        ],
        out_specs=[pl.BlockSpec((tm, c), row) for c in out_cols],
        compiler_params=_cparams(("arbitrary",)),
        name="even_proj",
    )(x, gpre, w1, gqa, gka, gcq, gckv, wuq, wukv, ca, sa, cb, sb)


def _attn_kernel(*refs, G, heads_per_kv, rope_ext, nk):
    if rope_ext:
        q_ref, qr_ref, k_ref, kr_ref, v_ref, o_ref = refs[:6]
        scratch = refs[6:]
    else:
        q_ref, k_ref, v_ref, o_ref = refs[:4]
        qr_ref = kr_ref = None
        scratch = refs[4:]
    ki = pl.program_id(3)
    tq = q_ref.shape[0]
    items = [(g, slice(r, r + ATTN_ROWS)) for g in range(G) for r in range(0, tq, ATTN_ROWS)]

    def scores(g, rows):
        sl = slice(g * 128, (g + 1) * 128)
        kvh = g // heads_per_kv
        q, k = q_ref[rows, sl], k_ref[:, kvh * 128:(kvh + 1) * 128]
        if rope_ext:
            q = jnp.concatenate([q, qr_ref[rows, sl]], axis=-1)
            k = jnp.concatenate([k, kr_ref[...]], axis=-1)
        return _dot_nt(q, k)

    n_kv = G // heads_per_kv
    ones = jnp.ones((v_ref.shape[0], 128), BF16)

    def values_and_ones(negate):
        ext = [jnp.concatenate([v_ref[:, h * 128:(h + 1) * 128], ones], axis=-1) for h in range(n_kv)]
        return [-e for e in ext] if negate else ext

    def normalised(o_and_sum):
        return (o_and_sum[:, :128] / o_and_sum[:, 128:]).astype(o_ref.dtype)

    if nk == 1:
        v_ext = values_and_ones(negate=False)
        for g, rows in items:
            s = scores(g, rows)
            p = jnp.exp2(s - jnp.max(s, axis=-1, keepdims=True))
            o_ref[rows, g * 128:(g + 1) * 128] = normalised(_dot(p.astype(BF16), v_ext[g // heads_per_kv]))
        return

    m_sc, acc_sc = scratch

    @pl.when(ki == 0)
    def _():
        m_sc[...] = jnp.full(m_sc.shape, -jnp.inf, F32)
        acc_sc[...] = jnp.zeros(acc_sc.shape, F32)

    neg_v_ext = values_and_ones(negate=True)
    for g, rows in items:
        s = scores(g, rows)
        m_prev = m_sc[g, rows]
        m_new = jnp.maximum(m_prev, jnp.max(s, axis=-1, keepdims=True))
        alpha = jnp.exp2(m_prev - m_new)
        p = jnp.exp2(s - m_new)
        acc_sc[g, rows] = alpha * acc_sc[g, rows] - _dot(p.astype(BF16), neg_v_ext[g // heads_per_kv])
        m_sc[g, rows] = m_new

    @pl.when(ki == nk - 1)
    def _():
        for g in range(G):
            o_ref[:, g * 128:(g + 1) * 128] = normalised(acc_sc[g])


def _attention(q, k, v, *, batch, seq, G, heads_per_kv, tq, tk, qr=None, kr=None):
    T, qw = q.shape
    n_groups = qw // (G * 128)
    kvw = G // heads_per_kv * 128
    assert kvw % 256 == 0
    nq, nk = seq // tq, seq // tk
    rope_ext = qr is not None
    q_map = lambda b, h, qi, ki: (b * nq + qi, h)
    kv_map = lambda b, h, qi, ki: (b * nk + ki, h)
    kr_map = lambda b, h, qi, ki: (b * nk + ki, 0)
    in_specs = [pl.BlockSpec((tq, G * 128), q_map)]
    args = [q]
    if rope_ext:
        in_specs.append(pl.BlockSpec((tq, G * 128), q_map))
        args.append(qr)
    in_specs.append(pl.BlockSpec((tk, kvw), kv_map))
    args.append(k)
    if rope_ext:
        in_specs.append(pl.BlockSpec((tk, 128), kr_map))
        args.append(kr)
    in_specs.append(pl.BlockSpec((tk, kvw), kv_map))
    args.append(v)
    scratch = []
    if nk > 1:
        scratch = [pltpu.VMEM((G, tq, 1), F32), pltpu.VMEM((G, tq, 256), F32)]
    return pl.pallas_call(
        functools.partial(_attn_kernel, G=G, heads_per_kv=heads_per_kv, rope_ext=rope_ext, nk=nk),
        out_shape=jax.ShapeDtypeStruct((T, qw), BF16),
        grid=(batch, n_groups, nq, nk),
        in_specs=in_specs,
        out_specs=pl.BlockSpec((tq, G * 128), q_map),
        scratch_shapes=scratch,
        compiler_params=_cparams(("arbitrary",) * 4),
        name="attn_mla" if rope_ext else "attn_gqa",
    )(*args)


def _out_proj_kernel(*refs, widths):
    x_ref, g_ref, w_ref = refs[:3]
    a_refs = refs[3:3 + len(widths)]
    o_ref = refs[3 + len(widths)]
    acc = None
    off = 0
    for a_ref, wd in zip(a_refs, widths):
        part = _dot(a_ref[...], w_ref[off:off + wd, :])
        acc = part if acc is None else acc + part
        off += wd
    o_ref[...] = x_ref[...] + _rms(acc, g_ref[...])


def _out_proj(x, g, w, parts, *, tm):
    T = x.shape[0]
    widths = tuple(wd for _, wd, _ in parts)
    row = lambda i: (i, 0)
    return pl.pallas_call(
        functools.partial(_out_proj_kernel, widths=widths),
        out_shape=jax.ShapeDtypeStruct((T, D_MODEL), F32),
        grid=(T // tm,),
        in_specs=[
            pl.BlockSpec((tm, D_MODEL), row),
            pl.BlockSpec((1, D_MODEL), lambda i: (0, 0)),
            pl.BlockSpec(w.shape, lambda i: (0, 0), pipeline_mode=pl.Buffered(1)),
        ] + [pl.BlockSpec((tm, wd), functools.partial(lambda i, cb: (i, cb), cb=cb)) for _, wd, cb in parts],
        out_specs=pl.BlockSpec((tm, D_MODEL), row),
        compiler_params=_cparams(("arbitrary",)),
        name="out_proj",
    )(x, g, w, *[a for a, _, _ in parts])


def _mlp_kernel(x_ref, gpre_ref, gpost_ref, wup_ref, wdn_ref, o_ref, hn_sc, *, nf):
    f = pl.program_id(1)
    pieces = [slice(r, r + NORM_ROWS) for r in range(0, x_ref.shape[0], NORM_ROWS)]

    @pl.when(f == 0)
    def _():
        for rows in pieces:
            hn_sc[rows, :] = _rms(x_ref[rows, :], gpre_ref[...]).astype(BF16)
            o_ref[rows, :] = jnp.zeros((NORM_ROWS, D_MODEL), F32)

    h = jnp.maximum(_dot(hn_sc[...], wup_ref[...]), 0.0)
    o_ref[...] += _dot((h * h).astype(BF16), wdn_ref[...])

    @pl.when(f == nf - 1)
    def _():
        for rows in pieces:
            o_ref[rows, :] = x_ref[rows, :] + _rms(o_ref[rows, :], gpost_ref[...])


def _mlp(x, gpre, gpost, wup, wdn, layer, *, tm, tf):
    T = x.shape[0]
    nf = D_FF // tf
    row = lambda i, f: (i, 0)
    fixed = lambda i, f: (0, 0)
    return pl.pallas_call(
        functools.partial(_mlp_kernel, nf=nf),
        out_shape=jax.ShapeDtypeStruct((T, D_MODEL), F32),
        grid=(T // tm, nf),
        in_specs=[
            pl.BlockSpec((tm, D_MODEL), row),
            pl.BlockSpec((1, D_MODEL), fixed),
            pl.BlockSpec((1, D_MODEL), fixed),
            pl.BlockSpec((None, D_MODEL, tf), lambda i, f: (layer, 0, f)),
            pl.BlockSpec((None, tf, D_MODEL), lambda i, f: (layer, f, 0)),
        ],
        out_specs=pl.BlockSpec((tm, D_MODEL), row),
        scratch_shapes=[pltpu.VMEM((tm, D_MODEL), BF16)],
        compiler_params=_cparams(("arbitrary", "arbitrary")),
        name="mlp",
    )(x, gpre, gpost, wup, wdn)


ODD_TN = 1024


def _odd_proj_kernel(x_ref, gpre_ref, w_ref, *rest, rope, n_q_blocks):
    if rope:
        c_ref, s_ref, o_ref, hn_sc = rest
    else:
        o_ref, hn_sc = rest
    j = pl.program_id(1)

    @pl.when(j == 0)
    def _():
        hn_sc[...] = _rms(x_ref[...], gpre_ref[...]).astype(BF16)

    acc = _dot(hn_sc[...], w_ref[...])
    if not rope:
        o_ref[...] = acc.astype(BF16)
        return
    scale = jnp.where(j < n_q_blocks, 1.0, C_QK_DIM ** -0.5).astype(F32)
    c = c_ref[...] * scale
    s = s_ref[...] * scale
    for h in range(ODD_TN // C_QK_DIM):
        x0 = acc[:, h * 256:h * 256 + 128]
        x1 = acc[:, h * 256 + 128:(h + 1) * 256]
        o_ref[:, h * 256:h * 256 + 128] = (x0 * c - x1 * s).astype(BF16)
        o_ref[:, h * 256 + 128:(h + 1) * 256] = (x0 * s + x1 * c).astype(BF16)


def _odd_proj(x, gpre, w, *, tm, seq=None, tables=None):
    T = x.shape[0]
    N = w.shape[1]
    rope = tables is not None
    in_specs = [
        pl.BlockSpec((tm, D_MODEL), lambda i, j: (i, 0)),
        pl.BlockSpec((1, D_MODEL), lambda i, j: (0, 0)),
        pl.BlockSpec((D_MODEL, ODD_TN), lambda i, j: (0, j)),
    ]
    args = [x, gpre, w]
    if rope:
        nblk_seq = seq // tm
        in_specs += [pl.BlockSpec((tm, 128), lambda i, j: (i % nblk_seq, 0))] * 2
        args += list(tables)
    return pl.pallas_call(
        functools.partial(_odd_proj_kernel, rope=rope, n_q_blocks=N // ODD_TN // 2),
        out_shape=jax.ShapeDtypeStruct((T, N), BF16),
        grid=(T // tm, N // ODD_TN),
        in_specs=in_specs,
        out_specs=pl.BlockSpec((tm, ODD_TN), lambda i, j: (i, j)),
        scratch_shapes=[pltpu.VMEM((tm, D_MODEL), BF16)],
        compiler_params=_cparams(("arbitrary", "arbitrary")),
        name="odd_qk_proj" if rope else "odd_vg_proj",
    )(*args)


def _ret_fwd_kernel(lgf_ref, lgb_ref, q_ref, k_ref, v_ref, y_ref, st_sc, d_sc, qd_sc, kd_sc, cd_sc, *, C, n_chunks):
    h = pl.program_id(1)
    lgf = lgf_ref[h]
    lgb = lgb_ref[h]

    @pl.when(pl.program_id(2) == 0)
    def _():
        st_sc[...] = jnp.zeros(st_sc.shape, F32)
        cd_sc[...] = jnp.exp(jnp.full(cd_sc.shape, C, F32) * lgf)
        i = lax.broadcasted_iota(jnp.int32, (C, C), 0)
        j = lax.broadcasted_iota(jnp.int32, (C, C), 1)
        rel = (i - j).astype(F32)
        d_sc[...] = jnp.where(rel >= 0, jnp.exp(rel * lgf), jnp.exp(-rel * lgb))
        idx = lax.broadcasted_iota(jnp.int32, (C, C_QK_DIM), 0).astype(F32)
        qd_sc[...] = jnp.exp((idx + 1.0) * lgf)
        kd_sc[...] = jnp.exp((C - 1.0 - idx) * lgf)

    st = st_sc[...]
    for c in range(n_chunks):
        rows = slice(c * C, (c + 1) * C)
        q, k, v = q_ref[rows, :], k_ref[rows, :], v_ref[rows, :]
        a = (_dot_nt(q, k) * d_sc[...]).astype(BF16)
        qd = (q.astype(F32) * qd_sc[...]).astype(BF16)
        y_ref[rows, :] = (_dot(a, v) + _dot(qd, st.astype(BF16))).astype(y_ref.dtype)
        kd = (k.astype(F32) * kd_sc[...]).astype(BF16)
        st = st * cd_sc[...] + _dot_tn(kd, v)
    st_sc[...] = st


def _ret_bwd_kernel(lgb_ref, q_ref, k_ref, v_ref, g_ref, yp_ref, o_ref, st_sc, qd_sc, kd_sc, cd_sc, *, C, n_chunks):
    h = pl.program_id(1)
    lgb = lgb_ref[h]

    @pl.when(pl.program_id(2) == 0)
    def _():
        st_sc[...] = jnp.zeros(st_sc.shape, F32)
        cd_sc[...] = jnp.exp(jnp.full(cd_sc.shape, C, F32) * lgb)
        idx = lax.broadcasted_iota(jnp.int32, (C, C_QK_DIM), 0).astype(F32)
        qd_sc[...] = jnp.exp((C - idx) * lgb)
        kd_sc[...] = jnp.exp(idx * lgb)

    st = st_sc[...]
    for c in reversed(range(n_chunks)):
        rows = slice(c * C, (c + 1) * C)
        q, k, v = q_ref[rows, :], k_ref[rows, :], v_ref[rows, :]
        qd = (q.astype(F32) * qd_sc[...]).astype(BF16)
        y = yp_ref[rows, :].astype(F32) + _dot(qd, st.astype(BF16))
        kd = (k.astype(F32) * kd_sc[...]).astype(BF16)
        st = st * cd_sc[...] + _dot_tn(kd, v)
        mu = jnp.mean(y, axis=-1, keepdims=True)
        yc = y - mu
        var = jnp.mean(yc * yc, axis=-1, keepdims=True)
        yn = yc * lax.rsqrt(var + GN_EPS)
        g = g_ref[rows, :].astype(F32)
        o_ref[rows, :] = (g * jax.nn.sigmoid(g) * yn).astype(BF16)
    st_sc[...] = st


def _retention(qk, vg, lgf, lgb, *, batch, seq, C, rows):
    T = qk.shape[0]
    nb = seq // rows
    n_chunks = rows // C
    H = C_HEADS
    fwd = lambda b, h, r: b * nb + r
    bwd = lambda b, h, r: b * nb + (nb - 1 - r)
    qk_blk = (rows, C_QK_DIM)
    v_blk = (rows, C_V_DIM)
    state_scratch = [pltpu.VMEM((C_QK_DIM, C_V_DIM), F32)]
    dec_scratch = [pltpu.VMEM((C, C_QK_DIM), F32), pltpu.VMEM((C, C_QK_DIM), F32), pltpu.VMEM((1, C_V_DIM), F32)]
    y_part = pl.pallas_call(
        functools.partial(_ret_fwd_kernel, C=C, n_chunks=n_chunks),
        out_shape=jax.ShapeDtypeStruct((T, H * C_V_DIM), BF16),
        grid_spec=pltpu.PrefetchScalarGridSpec(
            num_scalar_prefetch=2,
            grid=(batch, H, nb),
            in_specs=[
                pl.BlockSpec(qk_blk, lambda b, h, r, *_: (fwd(b, h, r), h)),
                pl.BlockSpec(qk_blk, lambda b, h, r, *_: (fwd(b, h, r), H + h)),
                pl.BlockSpec(v_blk, lambda b, h, r, *_: (fwd(b, h, r), h)),
            ],
            out_specs=pl.BlockSpec(v_blk, lambda b, h, r, *_: (fwd(b, h, r), h)),
            scratch_shapes=state_scratch + [pltpu.VMEM((C, C), F32)] + dec_scratch,
        ),
        compiler_params=_cparams(("arbitrary",) * 3),
        name="ret_fwd",
    )(lgf, lgb, qk, qk, vg)
    return pl.pallas_call(
        functools.partial(_ret_bwd_kernel, C=C, n_chunks=n_chunks),
        out_shape=jax.ShapeDtypeStruct((T, H * C_V_DIM), BF16),
        grid_spec=pltpu.PrefetchScalarGridSpec(
            num_scalar_prefetch=1,
            grid=(batch, H, nb),
            in_specs=[
                pl.BlockSpec(qk_blk, lambda b, h, r, *_: (bwd(b, h, r), h)),
                pl.BlockSpec(qk_blk, lambda b, h, r, *_: (bwd(b, h, r), H + h)),
                pl.BlockSpec(v_blk, lambda b, h, r, *_: (bwd(b, h, r), h)),
                pl.BlockSpec(v_blk, lambda b, h, r, *_: (bwd(b, h, r), H + h)),
                pl.BlockSpec(v_blk, lambda b, h, r, *_: (bwd(b, h, r), h)),
            ],
            out_specs=pl.BlockSpec(v_blk, lambda b, h, r, *_: (bwd(b, h, r), h)),
            scratch_shapes=state_scratch + dec_scratch,
        ),
        compiler_params=_cparams(("arbitrary",) * 3),
        name="ret_bwd",
    )(lgb, qk, qk, vg, vg, y_part)


def _deinterleave(d):
    return np.concatenate([np.arange(0, d, 2), np.arange(1, d, 2)])


def _rope_tables(seq_max):
    t = np.arange(seq_max)
    row = (t // GRID_W).astype(np.float32)
    col = (t % GRID_W).astype(np.float32)

    def ang(d_rot):
        n = d_rot // 4
        inv = jnp.asarray(ROPE_THETA, F32) ** (-jnp.arange(n, dtype=F32) / n)
        return jnp.concatenate([jnp.asarray(row)[:, None] * inv, jnp.asarray(col)[:, None] * inv], axis=-1)

    a = ang(A_HEAD_DIM)
    ca = jnp.concatenate([jnp.cos(a), jnp.cos(a)], axis=-1)
    sa = jnp.concatenate([-jnp.sin(a), jnp.sin(a)], axis=-1)
    b = ang(B_ROPE_DIM)
    z = jnp.zeros_like(b)
    cb = jnp.concatenate([jnp.cos(b), z, jnp.cos(b), z], axis=-1)
    sb = jnp.concatenate([-jnp.sin(b), z, jnp.sin(b), z], axis=-1)
    c = ang(C_QK_DIM)
    return ca, sa, cb, sb, jnp.cos(c), jnp.sin(c)


def _prep_even(w_in, g_qa, g_ka, g_cq, w_uq, g_ckv, w_ukv, w_out):
    perm_a = _deinterleave(A_HEAD_DIM)
    perm_b = _deinterleave(B_ROPE_DIM)
    w_in, w_uq, w_ukv = w_in.astype(BF16), w_uq.astype(BF16), w_ukv.astype(BF16)
    o = 0
    wqa = w_in[:, o:o + 1024].reshape(D_MODEL, A_HEADS, 128)[:, :, perm_a].reshape(D_MODEL, 1024)
    o += 1024
    wka = w_in[:, o:o + 256].reshape(D_MODEL, A_KV_HEADS, 128)[:, :, perm_a].reshape(D_MODEL, 256)
    o += 256
    wva = w_in[:, o:o + 256]
    o += 256
    wcq = w_in[:, o:o + 512]
    o += 512
    wckv = w_in[:, o:o + 256]
    o += 256
    wkr = w_in[:, o:o + 64][:, perm_b]
    z32 = jnp.zeros((D_MODEL, 32), w_in.dtype)
    wkr = jnp.concatenate([wkr[:, :32], z32, wkr[:, 32:], z32], axis=-1)
    w1 = jnp.concatenate([wqa, wka, wva, wcq, wckv, wkr], axis=-1).astype(BF16)
    uq = w_uq.reshape(B_Q_RANK, B_HEADS, B_NOPE_DIM + B_ROPE_DIM)
    uq_nope = uq[:, :, :B_NOPE_DIM].reshape(B_Q_RANK, 1024)
    uq_rope = uq[:, :, B_NOPE_DIM:][:, :, perm_b]
    zr = jnp.zeros((B_Q_RANK, B_HEADS, 32), w_uq.dtype)
    uq_rope = jnp.concatenate([uq_rope[:, :, :32], zr, uq_rope[:, :, 32:], zr], axis=-1).reshape(B_Q_RANK, 1024)
    wuq = jnp.concatenate([uq_nope, uq_rope], axis=-1).astype(BF16)
    ukv = w_ukv.reshape(B_KV_RANK, B_HEADS, B_NOPE_DIM + B_V_DIM)
    wukv = jnp.concatenate([ukv[:, :, :B_NOPE_DIM].reshape(B_KV_RANK, 1024),
                            ukv[:, :, B_NOPE_DIM:].reshape(B_KV_RANK, 1024)], axis=-1).astype(BF16)
    return dict(w1=w1, gqa=g_qa[perm_a][None], gka=g_ka[perm_a][None], gcq=g_cq[None], gckv=g_ckv[None],
                wuq=wuq, wukv=wukv, wout=w_out.astype(BF16))


def _prep_odd(w_in, decay_f, decay_b, w_out):
    perm_c = _deinterleave(C_QK_DIM)
    nqk = C_HEADS * C_QK_DIM
    w_in = w_in.astype(BF16)
    wq = w_in[:, :nqk].reshape(D_MODEL, C_HEADS, C_QK_DIM)[:, :, perm_c].reshape(D_MODEL, nqk)
    wk = w_in[:, nqk:2 * nqk].reshape(D_MODEL, C_HEADS, C_QK_DIM)[:, :, perm_c].reshape(D_MODEL, nqk)
    return dict(wqk=jnp.concatenate([wq, wk], axis=-1).astype(BF16), wvg=w_in[:, 2 * nqk:].astype(BF16),
                lgf=jax.nn.log_sigmoid(decay_f.astype(F32)), lgb=jax.nn.log_sigmoid(decay_b.astype(F32)),
                wout=w_out.astype(BF16))


def _trunk(x, *, batch, seq, tables, norms, even, odd, wup, wdn):
    ca, sa, cb, sb, cc, sc = tables
    g_mix_pre, g_mix_post, g_mlp_pre, g_mlp_post = norms
    for layer in range(DEPTH):
        gpre = g_mix_pre[layer][None]
        gpost = g_mix_post[layer][None]
        if layer % 2 == 0:
            e = even[layer // 2]
            qa, ka, va, qbn, qbr, kbn, vb, kr = _even_proj(
                x, gpre, e["w1"], e["gqa"], e["gka"], e["gcq"], e["gckv"], e["wuq"], e["wukv"],
                ca, sa, cb, sb, seq=seq, tm=512)
            tk, rows_per_step = 2048, 32 * ATTN_ROWS
            oa = _attention(qa, ka, va, batch=batch, seq=seq, G=A_HEADS, heads_per_kv=A_HEADS // A_KV_HEADS,
                            tq=rows_per_step // A_HEADS, tk=tk)
            ob = _attention(qbn, kbn, vb, batch=batch, seq=seq, G=B_HEADS // 2, heads_per_kv=1,
                            tq=rows_per_step // (B_HEADS // 2), tk=tk, qr=qbr, kr=kr)
            x = _out_proj(x, gpost, e["wout"], [(oa, 1024, 0), (ob, 1024, 0)], tm=512)
        else:
            o = odd[layer // 2]
            qk = _odd_proj(x, gpre, o["wqk"], tm=1024, seq=seq, tables=(cc, sc))
            vg = _odd_proj(x, gpre, o["wvg"], tm=1024)
            yg = _retention(qk, vg, o["lgf"], o["lgb"], batch=batch, seq=seq, C=256, rows=2048)
            x = _out_proj(x, gpost, o["wout"], [(yg, 4096, 0)], tm=512)
        x = _mlp(x, g_mlp_pre[layer][None], g_mlp_post[layer][None], wup, wdn, layer, tm=1024, tf=512)
    return x


def kernel(x_prompt, x_sample, g_mix_pre, g_mix_post, g_mlp_pre, g_mlp_post, w_in_even, g_qnorm_a, g_knorm_a,
           g_cq_b, w_uq_b, g_ckv_b, w_ukv_b, w_out_even, w_in_odd, decay_fwd, decay_bwd, w_out_odd,
           w_mlp_up, w_mlp_down):
    n_even, n_odd = w_in_even.shape[0], w_in_odd.shape[0]
    even = [_prep_even(w_in_even[e], g_qnorm_a[e], g_knorm_a[e], g_cq_b[e], w_uq_b[e], g_ckv_b[e], w_ukv_b[e],
                       w_out_even[e]) for e in range(n_even)]
    odd = [_prep_odd(w_in_odd[o], decay_fwd[o], decay_bwd[o], w_out_odd[o]) for o in range(n_odd)]
    wup = w_mlp_up.astype(BF16)
    wdn = w_mlp_down.astype(BF16)
    norms = (g_mix_pre, g_mix_post, g_mlp_pre, g_mlp_post)
    tables = _rope_tables(max(x_prompt.shape[1], x_sample.shape[1]))
    outs = []
    for xg in (x_prompt, x_sample):
        b, s, d = xg.shape
        y = _trunk(xg.reshape(b * s, d), batch=b, seq=s, tables=tables, norms=norms, even=even, odd=odd,
                   wup=wup, wdn=wdn)
        outs.append(y.reshape(b, s, d))
    return tuple(outs)
```
